```python
import math, functools
import jax, jax.numpy as jnp
from jax import lax
import numpy as np

D_MODEL = 4096
BATCH = 4
SEQ = 2048
DEPTH = 2
DEC_BATCH = 128
DEC_SEQ = 1
PAST_LEN = 16384
PAGE_SIZE = 128

MIX_WIDTH = D_MODEL
EPS = 1e-6
MLA_V = 128
MLA_HEADS = (MIX_WIDTH // 2) // MLA_V
MLA_WIDTH = MLA_HEADS * MLA_V
MLA_NOPE = 128
MLA_ROPE = 64
Q_LORA = 1024
KV_LORA = 512
ROPE_THETA = 10000.0
ATTN_QBLOCK = 128
GLA_WIDTH = MIX_WIDTH // 4
GLA_HEADS = 4
GLA_DV = GLA_WIDTH // GLA_HEADS
GLA_DK = GLA_DV // 2
GLA_LOWRANK = 16
GLA_GATE_NORM = 16.0
GLA_CHUNK = 32
SSM_WIDTH = MIX_WIDTH // 4
SSM_HEAD_DIM = 64
SSM_HEADS = SSM_WIDTH // SSM_HEAD_DIM
SSM_GROUPS = 2
SSM_D_STATE = 128
SSM_CONV = 4
SSM_CHUNK = 64
SSM_CONV_CH = SSM_WIDTH + 2 * SSM_GROUPS * SSM_D_STATE
IN_SIZES = (Q_LORA, KV_LORA + MLA_ROPE, GLA_HEADS * GLA_DK, GLA_HEADS * GLA_DK, GLA_WIDTH, GLA_LOWRANK,
            SSM_CONV_CH, SSM_HEADS, MLA_WIDTH, GLA_WIDTH, SSM_WIDTH)
IN_COLS = sum(IN_SIZES)

kernel_name = 'hymba_mla_gla_ssd_decoder_step'


def _offsets(sizes):
    out, acc = [], 0
    for s in sizes[:-1]:
        acc += s
        out.append(acc)
    return out


def rmsnorm(x, w):
    xf = x.astype(jnp.float32)
    y = xf * lax.rsqrt(jnp.mean(xf * xf, axis=-1, keepdims=True) + EPS)
    return (y * w.astype(jnp.float32)).astype(x.dtype)


def rope_cos_sin(pos):
    half = MLA_ROPE // 2
    inv_freq = ROPE_THETA ** (-jnp.arange(half, dtype=jnp.float32) / half)
    ang = pos.astype(jnp.float32)[:, None] * inv_freq[None, :]
    return jnp.cos(ang), jnp.sin(ang)


def apply_rope(x, cos, sin):
    half = x.shape[-1] // 2
    xf = x.astype(jnp.float32)
    x1, x2 = xf[..., :half], xf[..., half:]
    return jnp.concatenate([x1 * cos - x2 * sin, x2 * cos + x1 * sin], axis=-1).astype(x.dtype)


def _chunking(L, chunk):
    c = min(chunk, L)
    nc = -(-L // c)
    return c, nc, nc * c - L


def _pad_time(a, pad):
    return jnp.pad(a, [(0, 0), (0, pad)] + [(0, 0)] * (a.ndim - 2))


def segsum(x):
    T = x.shape[-1]
    xr = jnp.broadcast_to(x[..., None], x.shape + (T,))
    xr = jnp.where(jnp.tril(jnp.ones((T, T), dtype=bool), -1), xr, 0.0)
    xs = jnp.cumsum(xr, axis=-2)
    return jnp.where(jnp.tril(jnp.ones((T, T), dtype=bool)), xs, -jnp.inf)


def gla_chunked(q, k, v, log_a, s0):
    b, L, H, _ = q.shape
    dv = v.shape[-1]
    c, nc, pad = _chunking(L, GLA_CHUNK)

    def blocks(a):
        a = _pad_time(a.astype(jnp.float32), pad)
        return a.reshape(b, nc, c, H, a.shape[-1]).transpose(1, 0, 3, 2, 4)

    qc, kc, vc, ac = blocks(q), blocks(k), blocks(v), blocks(log_a)
    bc = jnp.cumsum(ac, axis=-2)
    mask = jnp.tril(jnp.ones((c, c), dtype=bool))

    def step(S, inp):
        qi, ki, vi, bi = inp
        q_dec = qi * jnp.exp(bi)
        k_dec = ki * jnp.exp(-bi)
        att = jnp.where(mask, jnp.einsum('bhid,bhjd->bhij', q_dec, k_dec), 0.0)
        o = jnp.einsum('bhid,bhde->bhie', q_dec, S) + jnp.einsum('bhij,bhje->bhie', att, vi)
        b_last = bi[:, :, -1:, :]
        S_new = (jnp.exp(b_last[:, :, 0, :])[..., None] * S
                 + jnp.einsum('bhcd,bhce->bhde', ki * jnp.exp(b_last - bi), vi))
        return S_new, o

    S_fin, o = lax.scan(step, s0.astype(jnp.float32), (qc, kc, vc, bc))
    o = o.transpose(1, 0, 3, 2, 4).reshape(b, nc * c, H, dv)[:, :L]
    return o, S_fin


def ssd_chunked(x, a, Bm, Cm, s0):
    b, L, h, p = x.shape
    c, nc, pad = _chunking(L, SSM_CHUNK)

    def blocks(t):
        t = _pad_time(t, pad)
        return t.reshape((b, nc, c) + t.shape[2:])

    x, a, Bm, Cm = blocks(x), blocks(a), blocks(Bm), blocks(Cm)
    a = a.transpose(0, 3, 1, 2)
    a_cs = jnp.cumsum(a, axis=-1)
    scores = jnp.einsum('bclhn,bcshn->bhcls', Cm, Bm) * jnp.exp(segsum(a))
    y_diag = jnp.einsum('bhcls,bcshp->bclhp', scores, x)
    decay_states = jnp.exp(a_cs[..., -1:] - a_cs)
    states = jnp.einsum('bclhn,bhcl,bclhp->bchpn', Bm, decay_states, x)
    states = jnp.concatenate([s0[:, None], states], axis=1)
    chunk_decay = jnp.exp(segsum(jnp.pad(a_cs[..., -1], ((0, 0), (0, 0), (1, 0)))))
    states = jnp.einsum('bhzc,bchpn->bzhpn', chunk_decay, states)
    y_off = jnp.einsum('bclhn,bchpn,bhcl->bclhp', Cm, states[:, :-1], jnp.exp(a_cs))
    y = (y_diag + y_off).reshape(b, nc * c, h, p)[:, :L]
    return y, states[:, -1]


def causal_conv(u, buf, w, bias):
    L = u.shape[1]
    up = jnp.concatenate([buf.astype(u.dtype), u], axis=1)
    out = bias
    for j in range(SSM_CONV):
        out = out + w[j] * up[:, j:j + L]
    return out, up[:, L:]


def mla_prompt_attend(q_nope, q_rope, c_kv, k_rope, w_kb, w_vb):
    b, L, H, _ = q_nope.shape
    k_nope = jnp.einsum('blc,chn->blhn', c_kv, w_kb)
    v = jnp.einsum('blc,chv->blhv', c_kv, w_vb)
    qb = min(ATTN_QBLOCK, L)
    nb = L // qb
    qn = q_nope.reshape(b, nb, qb, H, MLA_NOPE).transpose(1, 0, 2, 3, 4)
    qr = q_rope.reshape(b, nb, qb, H, MLA_ROPE).transpose(1, 0, 2, 3, 4)
    kpos = jnp.arange(L)
    scale = (MLA_NOPE + MLA_ROPE) ** -0.5

    def block(args):
        i, qn_b, qr_b = args
        s = (jnp.einsum('bqhn,bkhn->bhqk', qn_b, k_nope)
             + jnp.einsum('bqhr,bkr->bhqk', qr_b, k_rope)).astype(jnp.float32) * scale
        qpos = i * qb + jnp.arange(qb)
        s = jnp.where(kpos[None, :] <= qpos[:, None], s, -jnp.inf)
        pr = jax.nn.softmax(s, axis=-1).astype(v.dtype)
        return jnp.einsum('bhqk,bkhv->bqhv', pr, v)

    o = lax.map(block, (jnp.arange(nb), qn, qr))
    return o.transpose(1, 0, 2, 3, 4).reshape(b, L, H, MLA_V)


def mla_sample_attend(q_nope, q_rope, c_kv, k_rope, w_kb, w_vb, cache_lat, cache_kr, layer, page_table):
    f32 = jnp.float32
    scale = (MLA_NOPE + MLA_ROPE) ** -0.5
    S = c_kv.shape[1]
    q_lat = jnp.einsum('bshn,chn->bshc', q_nope, w_kb).astype(f32)
    q_r = q_rope.astype(f32)
    ckv_new = c_kv.astype(f32)
    s = (jnp.einsum('bshc,btc->bsht', q_lat, ckv_new)
         + jnp.einsum('bshr,btr->bsht', q_r, k_rope.astype(f32))) * scale
    causal = jnp.arange(S)[None, :] <= jnp.arange(S)[:, None]
    s = jnp.where(causal[None, :, None, :], s, -jnp.inf)
    m = jnp.max(s, axis=-1)
    pe = jnp.exp(s - m[..., None])
    l = jnp.sum(pe, axis=-1)
    acc = jnp.einsum('bsht,btc->bshc', pe, ckv_new)

    def step(carry, pages):
        m, l, acc = carry
        lat = cache_lat[layer, pages].astype(f32)
        kr = cache_kr[layer, pages].astype(f32)
        s = (jnp.einsum('bshc,bpc->bshp', q_lat, lat) + jnp.einsum('bshr,bpr->bshp', q_r, kr)) * scale
        m_new = jnp.maximum(m, jnp.max(s, axis=-1))
        corr = jnp.exp(m - m_new)
        pe = jnp.exp(s - m_new[..., None])
        return (m_new, l * corr + jnp.sum(pe, axis=-1),
                acc * corr[..., None] + jnp.einsum('bshp,bpc->bshc', pe, lat)), None

    (m, l, acc), _ = lax.scan(step, (m, l, acc), page_table.T)
    o_lat = (acc / l[..., None]).astype(c_kv.dtype)
    return jnp.einsum('bshc,chv->bshv', o_lat, w_vb)


def mixer_layer(h, pos, gla_s0, ssm_s0, conv_s0, attend, prm):
    dtype = h.dtype
    f32 = jnp.float32
    b, L, _ = h.shape
    u = rmsnorm(h, prm['norm_pre'])
    (q_a, kv_a, g_q, g_k, g_v, g_a, xbc, dt, gate_mla, gate_gla, z) = jnp.split(
        u @ prm['w_in'], _offsets(IN_SIZES), axis=-1)

    q = (rmsnorm(q_a, prm['mla_q_norm']) @ prm['mla_w_qb']).reshape(b, L, MLA_HEADS, MLA_NOPE + MLA_ROPE)
    cos, sin = rope_cos_sin(pos)
    q_nope = q[..., :MLA_NOPE]
    q_rope = apply_rope(q[..., MLA_NOPE:], cos[:, None, :], sin[:, None, :])
    c_kv = rmsnorm(kv_a[..., :KV_LORA], prm['mla_kv_norm'])
    k_rope = apply_rope(kv_a[..., KV_LORA:], cos, sin)
    o_mla = attend(q_nope, q_rope, c_kv, k_rope, prm['mla_w_kb'], prm['mla_w_vb'])
    o_mla = o_mla.reshape(b, L, MLA_WIDTH) * jax.nn.silu(gate_mla)

    gq = g_q.reshape(b, L, GLA_HEADS, GLA_DK).astype(f32) * (GLA_DK ** -0.5)
    gk = g_k.reshape(b, L, GLA_HEADS, GLA_DK)
    gv = g_v.reshape(b, L, GLA_HEADS, GLA_DV)
    log_a = (jax.nn.log_sigmoid((g_a @ prm['gla_w_a2'] + prm['gla_b_a']).astype(f32))
             / GLA_GATE_NORM).reshape(b, L, GLA_HEADS, GLA_DK)
    o_gla, gla_new = gla_chunked(gq, gk, gv, log_a, gla_s0)
    o_gla = rmsnorm(o_gla, prm['gla_norm']).reshape(b, L, GLA_WIDTH) * jax.nn.silu(gate_gla.astype(f32))

    xbc, conv_new = causal_conv(xbc, conv_s0, prm['ssm_conv_w'], prm['ssm_conv_b'])
    xbc = jax.nn.silu(xbc)
    xs, Bm, Cm = jnp.split(xbc, [SSM_WIDTH, SSM_WIDTH + SSM_GROUPS * SSM_D_STATE], axis=-1)
    xs = xs.reshape(b, L, SSM_HEADS, SSM_HEAD_DIM).astype(f32)
    rep = SSM_HEADS // SSM_GROUPS
    Bm = jnp.repeat(Bm.reshape(b, L, SSM_GROUPS, SSM_D_STATE), rep, axis=2).astype(f32)
    Cm = jnp.repeat(Cm.reshape(b, L, SSM_GROUPS, SSM_D_STATE), rep, axis=2).astype(f32)
    dt = jax.nn.softplus(dt.astype(f32) + prm['ssm_dt_bias'].astype(f32))
    A = -jnp.exp(prm['ssm_a_log'].astype(f32))
    y, ssm_new = ssd_chunked(xs * dt[..., None], A * dt, Bm, Cm, ssm_s0.astype(f32))
    y = y + prm['ssm_d'].astype(f32)[:, None] * xs
    yz = y.reshape(b, L, SSM_WIDTH) * jax.nn.silu(z.astype(f32))
    o_ssm = rmsnorm(yz.reshape(b, L, SSM_GROUPS, SSM_WIDTH // SSM_GROUPS),
                    prm['ssm_norm'].reshape(SSM_GROUPS, SSM_WIDTH // SSM_GROUPS)).reshape(b, L, SSM_WIDTH)

    mixed = jnp.concatenate([o_mla, o_gla.astype(dtype), o_ssm.astype(dtype)], axis=-1)
    out = h + rmsnorm(mixed @ prm['w_out'], prm['norm_post'])
    return out, (c_kv, k_rope, gla_new.astype(dtype), ssm_new.astype(dtype), conv_new)


def setup_inputs(seed: int = 0) -> dict:
    key = jax.random.key(seed)
    ks = jax.random.split(key, 32)
    f32 = jnp.float32
    n_pages = PAST_LEN // PAGE_SIZE
    n_used = DEC_BATCH * n_pages
    n_phys = n_used + n_used // 4

    def nrm(k, shape, scale=1.0):
        return jax.random.normal(k, shape, f32) * scale

    def gain(k, shape):
        return 1.0 + 0.02 * jax.random.normal(k, shape, f32)

    dt0 = jnp.exp(jax.random.uniform(ks[22], (DEPTH, SSM_HEADS), f32, math.log(1e-3), math.log(1e-1)))
    return {
        'x_prompt': nrm(ks[0], (BATCH, SEQ, D_MODEL)),
        'x_sample': nrm(ks[1], (DEC_BATCH, DEC_SEQ, D_MODEL)),
        'cache_mla_latent': nrm(ks[2], (DEPTH, n_phys, PAGE_SIZE, KV_LORA)),
        'cache_mla_krope': nrm(ks[3], (DEPTH, n_phys, PAGE_SIZE, MLA_ROPE)),
        'state_gla': nrm(ks[4], (DEPTH, DEC_BATCH, GLA_HEADS, GLA_DK, GLA_DV)),
        'state_ssm': nrm(ks[5], (DEPTH, DEC_BATCH, SSM_HEADS, SSM_HEAD_DIM, SSM_D_STATE)),
        'state_conv': nrm(ks[6], (DEPTH, DEC_BATCH, SSM_CONV - 1, SSM_CONV_CH)),
        'page_table': jax.random.permutation(ks[7], n_phys)[:n_used].reshape(DEC_BATCH, n_pages).astype(jnp.int32),
        'norm_pre': gain(ks[8], (DEPTH, D_MODEL)),
        'norm_post': gain(ks[9], (DEPTH, D_MODEL)),
        'w_in': nrm(ks[10], (DEPTH, D_MODEL, IN_COLS), D_MODEL ** -0.5),
        'mla_q_norm': gain(ks[11], (DEPTH, Q_LORA)),
        'mla_w_qb': nrm(ks[12], (DEPTH, Q_LORA, MLA_HEADS * (MLA_NOPE + MLA_ROPE)), Q_LORA ** -0.5),
        'mla_kv_norm': gain(ks[13], (DEPTH, KV_LORA)),
        'mla_w_kb': nrm(ks[14], (DEPTH, KV_LORA, MLA_HEADS, MLA_NOPE), KV_LORA ** -0.5),
        'mla_w_vb': nrm(ks[15], (DEPTH, KV_LORA, MLA_HEADS, MLA_V), KV_LORA ** -0.5),
        'gla_w_a2': nrm(ks[16], (DEPTH, GLA_LOWRANK, GLA_HEADS * GLA_DK), GLA_LOWRANK ** -0.5),
        'gla_b_a': nrm(ks[17], (DEPTH, GLA_HEADS * GLA_DK), 0.1),
        'gla_norm': gain(ks[18], (DEPTH, GLA_DV)),
        'ssm_conv_w': nrm(ks[19], (DEPTH, SSM_CONV, SSM_CONV_CH), SSM_CONV ** -0.5),
        'ssm_conv_b': nrm(ks[20], (DEPTH, SSM_CONV_CH), 0.02),
        'ssm_dt_bias': dt0 + jnp.log(-jnp.expm1(-dt0)),
        'ssm_a_log': jnp.log(jax.random.uniform(ks[21], (DEPTH, SSM_HEADS), f32, 1.0, 16.0)),
        'ssm_d': 1.0 + 0.1 * jax.random.normal(ks[23], (DEPTH, SSM_HEADS), f32),
        'ssm_norm': gain(ks[24], (DEPTH, SSM_WIDTH)),
        'w_out': nrm(ks[25], (DEPTH, MIX_WIDTH, D_MODEL), MIX_WIDTH ** -0.5),
    }


def reference(x_prompt, x_sample, cache_mla_latent, cache_mla_krope, state_gla, state_ssm, state_conv,
              page_table, norm_pre, norm_post, w_in, mla_q_norm, mla_w_qb, mla_kv_norm, mla_w_kb, mla_w_vb,
              gla_w_a2, gla_b_a, gla_norm, ssm_conv_w, ssm_conv_b, ssm_dt_bias, ssm_a_log, ssm_d, ssm_norm,
              w_out):
    f32 = jnp.float32
    bp, Lp = x_prompt.shape[0], x_prompt.shape[1]
    pos_p = jnp.arange(Lp)
    pos_s = page_table.shape[1] * PAGE_SIZE + jnp.arange(x_sample.shape[1])
    hp, hs = x_prompt, x_sample
    rows_p, rows_s = [], []
    for l in range(DEPTH):
        prm = {
            'norm_pre': norm_pre[l], 'norm_post': norm_post[l], 'w_in': w_in[l],
            'mla_q_norm': mla_q_norm[l], 'mla_w_qb': mla_w_qb[l], 'mla_kv_norm': mla_kv_norm[l],
            'mla_w_kb': mla_w_kb[l], 'mla_w_vb': mla_w_vb[l],
            'gla_w_a2': gla_w_a2[l], 'gla_b_a': gla_b_a[l], 'gla_norm': gla_norm[l],
            'ssm_conv_w': ssm_conv_w[l], 'ssm_conv_b': ssm_conv_b[l], 'ssm_dt_bias': ssm_dt_bias[l],
            'ssm_a_log': ssm_a_log[l], 'ssm_d': ssm_d[l], 'ssm_norm': ssm_norm[l], 'w_out': w_out[l],
        }
        hp, st_p = mixer_layer(
            hp, pos_p,
            jnp.zeros((bp, GLA_HEADS, GLA_DK, GLA_DV), f32),
            jnp.zeros((bp, SSM_HEADS, SSM_HEAD_DIM, SSM_D_STATE), f32),
            jnp.zeros((bp, SSM_CONV - 1, SSM_CONV_CH), x_prompt.dtype),
            mla_prompt_attend, prm)
        attend_s = functools.partial(mla_sample_attend, cache_lat=cache_mla_latent, cache_kr=cache_mla_krope,
                                     layer=l, page_table=page_table)
        hs, st_s = mixer_layer(hs, pos_s, state_gla[l], state_ssm[l], state_conv[l], attend_s, prm)
        rows_p.append(st_p)
        rows_s.append(st_s)

    new_latent_prompt = jnp.stack([r[0] for r in rows_p])
    new_krope_prompt = jnp.stack([r[1] for r in rows_p])
    new_gla_prompt = jnp.stack([r[2] for r in rows_p])
    new_ssm_prompt = jnp.stack([r[3] for r in rows_p])
    new_conv_prompt = jnp.stack([r[4] for r in rows_p])
    new_latent_sample = jnp.stack([r[0] for r in rows_s])
    new_krope_sample = jnp.stack([r[1] for r in rows_s])
    new_gla_sample = jnp.stack([r[2] for r in rows_s])
    new_ssm_sample = jnp.stack([r[3] for r in rows_s])
    new_conv_sample = jnp.stack([r[4] for r in rows_s])
    return (hp, hs, new_latent_prompt, new_krope_prompt, new_gla_prompt, new_ssm_prompt, new_conv_prompt,
            new_latent_sample, new_krope_sample, new_gla_sample, new_ssm_sample, new_conv_sample)
```

```python
import functools
import math

import jax
import jax.numpy as jnp
from jax import lax
from jax.experimental import pallas as pl
from jax.experimental.pallas import tpu as pltpu

F32 = jnp.float32
BF16 = jnp.bfloat16

EPS = 1e-6
MLA_V = 128
MLA_NOPE = 128
MLA_ROPE = 64
MLA_QPAD = 256
ROPE_THETA = 10000.0
GLA_HEADS = 4
GLA_GATE_NORM = 16.0
GLA_CHUNK = 32
SSM_HEAD_DIM = 64
SSM_GROUPS = 2
SSM_D_STATE = 128
SSM_CONV = 4
PAGE_SIZE = 128
LANES = 128
SMALL_W = 128
SMALL_KROPE = 0
SMALL_GA = 64
SMALL_DT = 80

GLA_TILE = 256
SSD_TILE = 256
FLASH_BQ = 256
FLASH_BK = 256
DEC_PAGES_PER_STEP = 8
DEC_ROWS = 8
VMEM_LIMIT = 52 * 1024 * 1024


def _cparams(*sem):
    return pltpu.CompilerParams(dimension_semantics=sem, vmem_limit_bytes=VMEM_LIMIT)


def _silu(x):
    return x * (1.0 / (1.0 + jnp.exp(-x)))


def _softplus(x):
    return jnp.maximum(x, 0.0) + jnp.log1p(jnp.exp(-jnp.abs(x)))


def _split3(x):
    a = x.astype(BF16)
    r = x - a.astype(F32)
    b = r.astype(BF16)
    c = (r - b.astype(F32)).astype(BF16)
    return a, b, c


def _tri_rows(t, block):
    i = lax.broadcasted_iota(jnp.int32, (t, t), 0)
    j = lax.broadcasted_iota(jnp.int32, (t, t), 1)
    return ((j <= i) & ((i // block) == (j // block))).astype(BF16)


def _tri_cols(t):
    i = lax.broadcasted_iota(jnp.int32, (t, t), 0)
    j = lax.broadcasted_iota(jnp.int32, (t, t), 1)
    return (i <= j).astype(BF16)


def _cumsum_rows(x, tri):
    a, b, c = _split3(x)
    d = functools.partial(jnp.dot, preferred_element_type=F32)
    return d(tri, a) + d(tri, b) + d(tri, c)


def _cumsum_lanes(x, tri_t):
    a, b, c = _split3(x)
    d = functools.partial(jnp.dot, preferred_element_type=F32)
    return d(a, tri_t) + d(b, tri_t) + d(c, tri_t)


def _dot_nt(a, b):
    return lax.dot_general(a, b, (((1,), (1,)), ((), ())), preferred_element_type=F32)


def _dot_tn(a, b):
    return lax.dot_general(a, b, (((0,), (0,)), ((), ())), preferred_element_type=F32)


def _col_bcast(row):
    return jnp.broadcast_to(row, (LANES, LANES)).T


def _rope128(x, tc, ta, tb):
    return x * tc + pltpu.roll(x, 96, axis=1) * ta + pltpu.roll(x, 32, axis=1) * tb


def _norm_small_kernel(x_ref, g_ref, ws_ref, u_ref, s_ref):
    x = x_ref[...]
    ms = jnp.mean(x * x, axis=-1, keepdims=True)
    u = (x * lax.rsqrt(ms + EPS) * g_ref[...]).astype(BF16)
    u_ref[...] = u
    s_ref[...] = jnp.dot(u, ws_ref[...], preferred_element_type=F32)


def _norm_small(x, g, w_small, bm):
    m, d = x.shape
    return pl.pallas_call(
        _norm_small_kernel,
        grid=(m // bm,),
        in_specs=[pl.BlockSpec((bm, d), lambda i: (i, 0)),
                  pl.BlockSpec((1, d), lambda i: (0, 0)),
                  pl.BlockSpec((d, SMALL_W), lambda i: (0, 0))],
        out_specs=[pl.BlockSpec((bm, d), lambda i: (i, 0)),
                   pl.BlockSpec((bm, SMALL_W), lambda i: (i, 0))],
        out_shape=[jax.ShapeDtypeStruct((m, d), BF16), jax.ShapeDtypeStruct((m, SMALL_W), F32)],
        compiler_params=_cparams("parallel"),
        name="norm_small",
    )(x, g, w_small)


def _mm_kernel(*refs, n):
    o_ref = refs[2 * n]
    acc = None
    for i in range(n):
        p = jnp.dot(refs[i][...].astype(BF16), refs[n + i][...], preferred_element_type=F32)
        acc = p if acc is None else acc + p
    o_ref[...] = acc.astype(o_ref.dtype)


def _matmul(lhs_list, w, k_sizes, bm, bn, out_dtype=F32):
    n = len(lhs_list)
    m = lhs_list[0].shape[0]
    ncols = w.shape[1]
    in_specs, k0 = [], 0
    for a in lhs_list:
        in_specs.append(pl.BlockSpec((bm, a.shape[1]), lambda i, j: (i, 0)))
    for ks in k_sizes:
        assert k0 % ks == 0
        in_specs.append(pl.BlockSpec((ks, bn), functools.partial(lambda i, j, kb: (kb, j), kb=k0 // ks)))
        k0 += ks
    return pl.pallas_call(
        functools.partial(_mm_kernel, n=n),
        grid=(m // bm, ncols // bn),
        in_specs=in_specs,
        out_specs=pl.BlockSpec((bm, bn), lambda i, j: (i, j)),
        out_shape=jax.ShapeDtypeStruct((m, ncols), out_dtype),
        compiler_params=_cparams("parallel", "parallel"),
        name="matmul%d" % n,
    )(*lhs_list, *([w] * n))


def _norm_mm_kernel(x_ref, g_ref, w_ref, o_ref):
    x = x_ref[...]
    ms = jnp.mean(x * x, axis=-1, keepdims=True)
    xn = (x * lax.rsqrt(ms + EPS) * g_ref[...]).astype(BF16)
    o_ref[...] = jnp.dot(xn, w_ref[...], preferred_element_type=F32)


def _norm_matmul(p, col_blk, g, w, bm, bn):
    m = p.shape[0]
    k, ncols = w.shape
    return pl.pallas_call(
        _norm_mm_kernel,
        grid=(m // bm, ncols // bn),
        in_specs=[pl.BlockSpec((bm, k), lambda i, j: (i, col_blk)),
                  pl.BlockSpec((1, k), lambda i, j: (0, 0)),
                  pl.BlockSpec((k, bn), lambda i, j: (0, j))],
        out_specs=pl.BlockSpec((bm, bn), lambda i, j: (i, j)),
        out_shape=jax.ShapeDtypeStruct((m, ncols), F32),
        compiler_params=_cparams("parallel", "parallel"),
        name="q_proj",
    )(p, g, w)


def _mla_kv_kernel(ckv_ref, small_ref, g_ref, tc_ref, ta_ref, tb_ref, *rest, expand, heads):
    if expand:
        wk_ref, wv_ref, lat_ref, kr_ref, kcat_ref, v_ref = rest
    else:
        lat_ref, kr_ref = rest
    x = ckv_ref[...]
    ms = jnp.mean(x * x, axis=-1, keepdims=True)
    c = x * lax.rsqrt(ms + EPS) * g_ref[...]
    lat_ref[...] = c
    kr = _rope128(small_ref[...], tc_ref[...], ta_ref[...], tb_ref[...])
    kr_ref[...] = kr[:, :MLA_ROPE]
    if expand:
        cb = c.astype(BF16)
        kn = jnp.dot(cb, wk_ref[...], preferred_element_type=F32).astype(BF16)
        v_ref[...] = jnp.dot(cb, wv_ref[...], preferred_element_type=F32).astype(BF16)
        krb = kr.astype(BF16)
        for h in range(heads):
            kcat_ref[:, h * MLA_QPAD:h * MLA_QPAD + MLA_NOPE] = kn[:, h * MLA_NOPE:(h + 1) * MLA_NOPE]
            kcat_ref[:, h * MLA_QPAD + MLA_NOPE:(h + 1) * MLA_QPAD] = krb


def _mla_kv(p, ckv_blk, small, g, tabs, wk, wv, bm, expand):
    m = p.shape[0]
    kvl = g.shape[1]
    nt = tabs[0].shape[0] // bm
    heads = wk.shape[1] // MLA_NOPE
    tab_spec = pl.BlockSpec((bm, LANES), lambda i: (i % nt, 0))
    in_specs = [pl.BlockSpec((bm, kvl), lambda i: (i, ckv_blk)),
                pl.BlockSpec((bm, SMALL_W), lambda i: (i, 0)),
                pl.BlockSpec((1, kvl), lambda i: (0, 0)),
                tab_spec, tab_spec, tab_spec]
    out_specs = [pl.BlockSpec((bm, kvl), lambda i: (i, 0)),
                 pl.BlockSpec((bm, MLA_ROPE), lambda i: (i, 0))]
    out_shape = [jax.ShapeDtypeStruct((m, kvl), F32), jax.ShapeDtypeStruct((m, MLA_ROPE), F32)]
    args = [p, small, g, *tabs]
    if expand:
        in_specs += [pl.BlockSpec(wk.shape, lambda i: (0, 0)), pl.BlockSpec(wv.shape, lambda i: (0, 0))]
        out_specs += [pl.BlockSpec((bm, heads * MLA_QPAD), lambda i: (i, 0)),
                      pl.BlockSpec((bm, heads * MLA_V), lambda i: (i, 0))]
        out_shape += [jax.ShapeDtypeStruct((m, heads * MLA_QPAD), BF16),
                      jax.ShapeDtypeStruct((m, heads * MLA_V), BF16)]
        args += [wk, wv]
    return pl.pallas_call(
        functools.partial(_mla_kv_kernel, expand=expand, heads=heads),
        grid=(m // bm,),
        in_specs=in_specs, out_specs=out_specs, out_shape=out_shape,
        compiler_params=_cparams("parallel"),
        name="mla_kv",
    )(*args)


def _flash_kernel(q_ref, k_ref, v_ref, gate_ref, tc_ref, ta_ref, tb_ref, o_ref, qs_ref, m_ref, l_ref, acc_ref,
                  *, bq, bk, scale):
    qi = pl.program_id(2)
    q = q_ref[...]
    qr = _rope128(q[:, MLA_NOPE:], tc_ref[...], ta_ref[...], tb_ref[...])
    qs_ref[:, :MLA_NOPE] = (q[:, :MLA_NOPE] * scale).astype(BF16)
    qs_ref[:, MLA_NOPE:] = (qr * scale).astype(BF16)
    m_ref[...] = jnp.full(m_ref.shape, -jnp.inf, F32)
    l_ref[...] = jnp.zeros(l_ref.shape, F32)
    acc_ref[...] = jnp.zeros(acc_ref.shape, F32)
    qpos = qi * bq + lax.broadcasted_iota(jnp.int32, (bq, bk), 0)
    n_kv = ((qi + 1) * bq + bk - 1) // bk

    def body(j, carry):
        k = k_ref[pl.ds(pl.multiple_of(j * bk, bk), bk), :]
        v = v_ref[pl.ds(pl.multiple_of(j * bk, bk), bk), :]
        s = _dot_nt(qs_ref[...], k)
        kpos = j * bk + lax.broadcasted_iota(jnp.int32, (bq, bk), 1)
        s = jnp.where(kpos <= qpos, s, -jnp.inf)
        m_prev = m_ref[...]
        m_new = jnp.maximum(m_prev, jnp.max(s, axis=1, keepdims=True))
        alpha = jnp.exp(m_prev - m_new)
        pr = jnp.exp(s - m_new)
        l_ref[...] = alpha * l_ref[...] + jnp.sum(pr, axis=1, keepdims=True)
        acc_ref[...] = alpha * acc_ref[...] + jnp.dot(pr.astype(BF16), v, preferred_element_type=F32)
        m_ref[...] = m_new
        return carry

    lax.fori_loop(0, n_kv, body, 0)
    o = acc_ref[...] * (1.0 / l_ref[...])
    o_ref[...] = (o * _silu(gate_ref[...])).astype(o_ref.dtype)


def _flash(q, kcat, v, p, gate_blk0, tabs, batch, seq, heads):
    m = q.shape[0]
    bq, bk = min(FLASH_BQ, seq), min(FLASH_BK, seq)
    nq = seq // bq
    scale = (MLA_NOPE + MLA_ROPE) ** -0.5
    tab_spec = pl.BlockSpec((bq, LANES), lambda b, h, i: (i, 0))
    return pl.pallas_call(
        functools.partial(_flash_kernel, bq=bq, bk=bk, scale=scale),
        grid=(batch, heads, nq),
        in_specs=[pl.BlockSpec((bq, MLA_QPAD), lambda b, h, i: (b * nq + i, h)),
                  pl.BlockSpec((seq, MLA_QPAD), lambda b, h, i: (b, h)),
                  pl.BlockSpec((seq, MLA_V), lambda b, h, i: (b, h)),
                  pl.BlockSpec((bq, MLA_V), lambda b, h, i: (b * nq + i, gate_blk0 + h)),
                  tab_spec, tab_spec, tab_spec],
        out_specs=pl.BlockSpec((bq, MLA_V), lambda b, h, i: (b * nq + i, h)),
        out_shape=jax.ShapeDtypeStruct((m, heads * MLA_V), BF16),
        scratch_shapes=[pltpu.VMEM((bq, MLA_QPAD), BF16), pltpu.VMEM((bq, 1), F32),
                        pltpu.VMEM((bq, 1), F32), pltpu.VMEM((bq, MLA_V), F32)],
        compiler_params=_cparams("parallel", "parallel", "arbitrary"),
        name="flash",
    )(q, kcat, v, p, *tabs)


def _log_decay(small, wa_ref, ba_ref):
    x = jnp.dot(small.astype(BF16), wa_ref[...], preferred_element_type=F32) + ba_ref[...]
    return -_softplus(-x) * (1.0 / GLA_GATE_NORM)


def _gla_kernel(q_ref, k_ref, v_ref, small_ref, gate_ref, wa_ref, ba_ref, gn_ref, o_ref, st_ref, s_ref,
                *, tile, dk, dv):
    c = pl.program_id(2)

    @pl.when(c == 0)
    def _():
        s_ref[...] = jnp.zeros(s_ref.shape, F32)

    la = _log_decay(small_ref[...], wa_ref, ba_ref)
    bcs = _cumsum_rows(la, _tri_rows(tile, GLA_CHUNK))
    ii = lax.broadcasted_iota(jnp.int32, (GLA_CHUNK, GLA_CHUNK), 0)
    jj = lax.broadcasted_iota(jnp.int32, (GLA_CHUNK, GLA_CHUNK), 1)
    scale = dk ** -0.5
    outs = []
    for i in range(tile // GLA_CHUNK):
        sl = slice(i * GLA_CHUNK, (i + 1) * GLA_CHUNK)
        bi = bcs[sl]
        ki = k_ref[sl, :]
        vi = v_ref[sl, :].astype(BF16)
        qd = (q_ref[sl, :] * scale * jnp.exp(bi)).astype(BF16)
        kd = (ki * jnp.exp(-bi)).astype(BF16)
        att = jnp.where(jj <= ii, _dot_nt(qd, kd), 0.0).astype(BF16)
        s_old = s_ref[...]
        o = (jnp.dot(qd, s_old.astype(BF16), preferred_element_type=F32)
             + jnp.dot(att, vi, preferred_element_type=F32))
        outs.append(o)
        bl = bi[GLA_CHUNK - 1:GLA_CHUNK, :]
        kv = _dot_tn((ki * jnp.exp(bl - bi)).astype(BF16), vi)
        dcol = _col_bcast(jnp.exp(bl))
        for j in range(dv // LANES):
            ls = slice(j * LANES, (j + 1) * LANES)
            s_ref[:, ls] = dcol * s_old[:, ls] + kv[:, ls]
    o = jnp.concatenate(outs, axis=0)
    ms = jnp.mean(o * o, axis=-1, keepdims=True)
    o = o * lax.rsqrt(ms + EPS) * gn_ref[...]
    o_ref[...] = (o * _silu(gate_ref[...])).astype(o_ref.dtype)

    @pl.when(c == pl.num_programs(2) - 1)
    def _():
        st_ref[0, 0] = s_ref[...]


def _gla_prompt(p, small, blk, wa_pad, ba, gn, batch, seq, dk, dv):
    m = p.shape[0]
    tile = min(GLA_TILE, seq)
    nb = seq // tile
    row = lambda b, h, c: b * nb + c
    return pl.pallas_call(
        functools.partial(_gla_kernel, tile=tile, dk=dk, dv=dv),
        grid=(batch, GLA_HEADS, nb),
        in_specs=[pl.BlockSpec((tile, dk), lambda b, h, c: (row(b, h, c), blk["g_q"] + h)),
                  pl.BlockSpec((tile, dk), lambda b, h, c: (row(b, h, c), blk["g_k"] + h)),
                  pl.BlockSpec((tile, dv), lambda b, h, c: (row(b, h, c), blk["g_v"] + h)),
                  pl.BlockSpec((tile, SMALL_W), lambda b, h, c: (row(b, h, c), 0)),
                  pl.BlockSpec((tile, dv), lambda b, h, c: (row(b, h, c), blk["gate_gla"] + h)),
                  pl.BlockSpec((SMALL_W, dk), lambda b, h, c: (0, h)),
                  pl.BlockSpec((1, dk), lambda b, h, c: (0, h)),
                  pl.BlockSpec((1, dv), lambda b, h, c: (0, 0))],
        out_specs=[pl.BlockSpec((tile, dv), lambda b, h, c: (row(b, h, c), h)),
                   pl.BlockSpec((1, 1, dk, dv), lambda b, h, c: (b, h, 0, 0))],
        out_shape=[jax.ShapeDtypeStruct((m, GLA_HEADS * dv), BF16),
                   jax.ShapeDtypeStruct((batch, GLA_HEADS, dk, dv), F32)],
        scratch_shapes=[pltpu.VMEM((dk, dv), F32)],
        compiler_params=_cparams("parallel", "parallel", "arbitrary"),
        name="gla_prompt",
    )(p, p, p, small, p, wa_pad, ba, gn)


def _gla_dec_kernel(q_ref, k_ref, v_ref, small_ref, gate_ref, wa_ref, ba_ref, gn_ref, s0_ref, o_ref, st_ref, ob_ref,
                    *, dk, dv):
    la = _log_decay(small_ref[...], wa_ref, ba_ref)
    a = jnp.exp(la)
    q = q_ref[...] * (dk ** -0.5)
    k = k_ref[...]
    v = v_ref[...]
    for r in range(DEC_ROWS):
        for h in range(GLA_HEADS):
            ks = slice(h * dk, (h + 1) * dk)
            acol = _col_bcast(a[r:r + 1, ks])
            kcol = _col_bcast(k[r:r + 1, ks])
            qcol = _col_bcast(q[r:r + 1, ks])
            for j in range(dv // LANES):
                ls = slice(j * LANES, (j + 1) * LANES)
                vrow = v[r:r + 1, h * dv + j * LANES:h * dv + (j + 1) * LANES]
                s_new = acol * s0_ref[r, h, :, ls] + kcol * vrow
                st_ref[r, h, :, ls] = s_new
                ob_ref[r:r + 1, h * dv + j * LANES:h * dv + (j + 1) * LANES] = jnp.sum(
                    qcol * s_new, axis=0, keepdims=True)
    for h in range(GLA_HEADS):
        vs = slice(h * dv, (h + 1) * dv)
        o = ob_ref[:, vs]
        ms = jnp.mean(o * o, axis=-1, keepdims=True)
        o_ref[:, vs] = o * lax.rsqrt(ms + EPS) * gn_ref[...] * _silu(gate_ref[:, vs])


def _gla_dec(p, small, blk, wa_pad, ba, gn, s0, dk, dv):
    m = p.shape[0]
    hk, hv = GLA_HEADS * dk, GLA_HEADS * dv
    return pl.pallas_call(
        functools.partial(_gla_dec_kernel, dk=dk, dv=dv),
        grid=(m // DEC_ROWS,),
        in_specs=[pl.BlockSpec((DEC_ROWS, hk), lambda i: (i, blk["g_q"] * dk // hk)),
                  pl.BlockSpec((DEC_ROWS, hk), lambda i: (i, blk["g_k"] * dk // hk)),
                  pl.BlockSpec((DEC_ROWS, hv), lambda i: (i, blk["g_v"] * dv // hv)),
                  pl.BlockSpec((DEC_ROWS, SMALL_W), lambda i: (i, 0)),
                  pl.BlockSpec((DEC_ROWS, hv), lambda i: (i, blk["gate_gla"] * dv // hv)),
                  pl.BlockSpec((SMALL_W, hk), lambda i: (0, 0)),
                  pl.BlockSpec((1, hk), lambda i: (0, 0)),
                  pl.BlockSpec((1, dv), lambda i: (0, 0)),
                  pl.BlockSpec((DEC_ROWS, GLA_HEADS, dk, dv), lambda i: (i, 0, 0, 0))],
        out_specs=[pl.BlockSpec((DEC_ROWS, hv), lambda i: (i, 0)),
                   pl.BlockSpec((DEC_ROWS, GLA_HEADS, dk, dv), lambda i: (i, 0, 0, 0))],
        out_shape=[jax.ShapeDtypeStruct((m, hv), F32),
                   jax.ShapeDtypeStruct(s0.shape, F32)],
        scratch_shapes=[pltpu.VMEM((DEC_ROWS, hv), F32)],
        compiler_params=_cparams("parallel"),
        name="gla_dec",
    )(p, p, p, small, p, wa_pad, ba, gn, s0)


def _group_norm_gate(y, z, gn, groups):
    w = y.shape[1] // groups
    yz = y * _silu(z)
    outs = []
    for g in range(groups):
        t = yz[:, g * w:(g + 1) * w]
        ms = jnp.mean(t * t, axis=-1, keepdims=True)
        outs.append(t * lax.rsqrt(ms + EPS) * gn[:, g * w:(g + 1) * w])
    return outs


def _ssd_kernel(xbc_ref, small_ref, dtt_ref, z_ref, cw_ref, cb_ref, dtb_ref, dtbc_ref, al_ref, alc_ref, dx_ref,
                gn_ref, conv0_ref, s0_ref, o_ref, st_ref, cv_ref, s_ref, up_ref, y_ref, *, tile, heads, width):
    c = pl.program_id(1)
    taps = SSM_CONV
    base = 8

    @pl.when(c == 0)
    def _():
        s_ref[...] = s0_ref[0]
        up_ref[base - (taps - 1):base, :] = conv0_ref[0]

    up_ref[base:base + tile, :] = xbc_ref[...]
    conv = cb_ref[...]
    for j in range(taps):
        conv = conv + cw_ref[j:j + 1, :] * up_ref[base - (taps - 1) + j:base - (taps - 1) + j + tile, :]
    tail = up_ref[base + tile - (taps - 1):base + tile, :]
    up_ref[base - (taps - 1):base, :] = tail

    @pl.when(c == pl.num_programs(1) - 1)
    def _():
        cv_ref[0] = tail

    xbc = _silu(conv)
    ns = SSM_D_STATE
    xs = xbc[:, :width]
    bm = xbc[:, width:width + SSM_GROUPS * ns].astype(BF16)
    cm = xbc[:, width + SSM_GROUPS * ns:].astype(BF16)
    dt = _softplus(small_ref[:, SMALL_DT:SMALL_DT + heads] + dtb_ref[...])
    a = -jnp.exp(al_ref[...]) * dt
    tri = _tri_rows(tile, tile)
    acs = _cumsum_rows(a, tri)
    dtt = _softplus(dtt_ref[0] + dtbc_ref[...])
    acs_t = _cumsum_lanes(-jnp.exp(alc_ref[...]) * dtt, _tri_cols(tile))
    a_last = acs[tile - 1:tile, :]
    e_cs = jnp.exp(acs)
    dec = jnp.exp(a_last - acs)
    e_last = jnp.exp(a_last)
    ii = lax.broadcasted_iota(jnp.int32, (tile, tile), 0)
    jj = lax.broadcasted_iota(jnp.int32, (tile, tile), 1)
    low = jj <= ii
    hp = SSM_HEAD_DIM
    per_group = heads // SSM_GROUPS
    scores = [_dot_nt(cm[:, g * ns:(g + 1) * ns], bm[:, g * ns:(g + 1) * ns]) for g in range(SSM_GROUPS)]
    for h in range(heads):
        g = h // per_group
        col = jnp.broadcast_to(acs[:, h:h + 1], (tile, tile))
        row = jnp.broadcast_to(acs_t[h:h + 1, :], (tile, tile))
        lm = jnp.where(low, jnp.exp(jnp.where(low, col - row, 0.0)), 0.0)
        xh = xs[:, h * hp:(h + 1) * hp]
        xdt = xh * jnp.broadcast_to(dt[:, h:h + 1], (tile, hp))
        y = jnp.dot((scores[g] * lm).astype(BF16), xdt.astype(BF16), preferred_element_type=F32)
        sh = s_ref[h]
        y = y + (_dot_nt(cm[:, g * ns:(g + 1) * ns], sh.astype(BF16))
                 * jnp.broadcast_to(e_cs[:, h:h + 1], (tile, hp)))
        st = _dot_tn((xdt * jnp.broadcast_to(dec[:, h:h + 1], (tile, hp))).astype(BF16),
                     bm[:, g * ns:(g + 1) * ns])
        s_ref[h] = jnp.broadcast_to(e_last[:, h:h + 1], (hp, ns)) * sh + st
        y_ref[:, h * hp:(h + 1) * hp] = y + dx_ref[:, h * hp:(h + 1) * hp] * xh
    outs = _group_norm_gate(y_ref[...], z_ref[...], gn_ref[...], SSM_GROUPS)
    gw = width // SSM_GROUPS
    for g in range(SSM_GROUPS):
        o_ref[:, g * gw:(g + 1) * gw] = outs[g].astype(o_ref.dtype)

    @pl.when(c == pl.num_programs(1) - 1)
    def _():
        st_ref[0] = s_ref[...]


def _ssd_prompt(p, small, dtt, blk, sw, conv0, s0, batch, seq):
    m = p.shape[0]
    heads = sw["dtb"].shape[1]
    width = heads * SSM_HEAD_DIM
    ch = sw["cw"].shape[1]
    tile = min(SSD_TILE, seq)
    nb = seq // tile
    full = lambda a: pl.BlockSpec(a.shape, lambda b, c: (0,) * a.ndim)
    return pl.pallas_call(
        functools.partial(_ssd_kernel, tile=tile, heads=heads, width=width),
        grid=(batch, nb),
        in_specs=[pl.BlockSpec((tile, ch), lambda b, c: (b * nb + c, blk["xbc"])),
                  pl.BlockSpec((tile, SMALL_W), lambda b, c: (b * nb + c, 0)),
                  pl.BlockSpec((1, heads, tile), lambda b, c: (b, 0, c)),
                  pl.BlockSpec((tile, width), lambda b, c: (b * nb + c, blk["z"])),
                  full(sw["cw"]), full(sw["cb"]), full(sw["dtb"]), full(sw["dtbc"]), full(sw["al"]),
                  full(sw["alc"]), full(sw["dx"]), full(sw["gn"]),
                  pl.BlockSpec((1, SSM_CONV - 1, ch), lambda b, c: (b, 0, 0)),
                  pl.BlockSpec((1, heads, SSM_HEAD_DIM, SSM_D_STATE), lambda b, c: (b, 0, 0, 0))],
        out_specs=[pl.BlockSpec((tile, width), lambda b, c: (b * nb + c, 0)),
                   pl.BlockSpec((1, heads, SSM_HEAD_DIM, SSM_D_STATE), lambda b, c: (b, 0, 0, 0)),
                   pl.BlockSpec((1, SSM_CONV - 1, ch), lambda b, c: (b, 0, 0))],
        out_shape=[jax.ShapeDtypeStruct((m, width), BF16),
                   jax.ShapeDtypeStruct(s0.shape, F32),
                   jax.ShapeDtypeStruct(conv0.shape, F32)],
        scratch_shapes=[pltpu.VMEM((heads, SSM_HEAD_DIM, SSM_D_STATE), F32),
                        pltpu.VMEM((8 + tile, ch), F32),
                        pltpu.VMEM((tile, width), F32)],
        compiler_params=_cparams("parallel", "arbitrary"),
        name="ssd_prompt",
    )(p, small, dtt, p, sw["cw"], sw["cb"], sw["dtb"], sw["dtbc"], sw["al"], sw["alc"], sw["dx"], sw["gn"],
      conv0, s0)


def _ssd_dec_kernel(xbc_ref, dtx_ref, z_ref, cw_ref, cb_ref, dtbx_ref, alx_ref, dx_ref, gn_ref, conv0_ref, s0_ref,
                    o_ref, st_ref, cv_ref, y_ref, *, heads, width):
    taps = SSM_CONV
    ch = xbc_ref.shape[1]
    ns = SSM_D_STATE
    xr = xbc_ref[...]
    conv = cb_ref[...] + cw_ref[taps - 1:taps, :] * xr
    for j in range(taps - 1):
        conv = conv + cw_ref[j:j + 1, :] * conv0_ref[:, j * ch:(j + 1) * ch]
    for j in range(taps - 2):
        cv_ref[:, j * ch:(j + 1) * ch] = conv0_ref[:, (j + 1) * ch:(j + 2) * ch]
    cv_ref[:, (taps - 2) * ch:(taps - 1) * ch] = xr
    xbc = _silu(conv)
    xs = xbc[:, :width]
    bm = xbc[:, width:width + SSM_GROUPS * ns]
    cm = xbc[:, width + SSM_GROUPS * ns:].astype(BF16)
    dt = _softplus(dtx_ref[...] + dtbx_ref[...])
    da = jnp.exp(-jnp.exp(alx_ref[...]) * dt)
    xdt = xs * dt
    per_group = width // SSM_GROUPS
    for r in range(DEC_ROWS):
        for cidx in range(width // LANES):
            ls = slice(cidx * LANES, (cidx + 1) * LANES)
            g = (cidx * LANES) // per_group
            hs = cidx * (LANES // SSM_HEAD_DIM)
            dcol = _col_bcast(da[r:r + 1, ls])
            xcol = _col_bcast(xdt[r:r + 1, ls])
            parts = []
            for t in range(LANES // SSM_HEAD_DIM):
                sl = slice(t * SSM_HEAD_DIM, (t + 1) * SSM_HEAD_DIM)
                s_new = dcol[sl, :] * s0_ref[r, hs + t] + xcol[sl, :] * bm[r:r + 1, g * ns:(g + 1) * ns]
                st_ref[r, hs + t] = s_new
                parts.append(s_new)
            s_cat = jnp.concatenate(parts, axis=0).astype(BF16)
            crow = jnp.broadcast_to(cm[r:r + 1, g * ns:(g + 1) * ns], (8, ns))
            y_ref[r:r + 1, ls] = _dot_nt(crow, s_cat)[0:1, :]
    y = y_ref[...] + dx_ref[...] * xs
    outs = _group_norm_gate(y, z_ref[...], gn_ref[...], SSM_GROUPS)
    gw = width // SSM_GROUPS
    for g in range(SSM_GROUPS):
        o_ref[:, g * gw:(g + 1) * gw] = outs[g]


def _ssd_dec(p, dtx, blk, sw, conv0, s0):
    m = p.shape[0]
    heads = sw["dtb"].shape[1]
    width = heads * SSM_HEAD_DIM
    ch = sw["cw"].shape[1]
    full = lambda a: pl.BlockSpec(a.shape, lambda i: (0,) * a.ndim)
    return pl.pallas_call(
        functools.partial(_ssd_dec_kernel, heads=heads, width=width),
        grid=(m // DEC_ROWS,),
        in_specs=[pl.BlockSpec((DEC_ROWS, ch), lambda i: (i, blk["xbc"])),
                  pl.BlockSpec((DEC_ROWS, width), lambda i: (i, 0)),
                  pl.BlockSpec((DEC_ROWS, width), lambda i: (i, blk["z"])),
                  full(sw["cw"]), full(sw["cb"]), full(sw["dtbx"]), full(sw["alx"]), full(sw["dx"]), full(sw["gn"]),
                  pl.BlockSpec((DEC_ROWS, (SSM_CONV - 1) * ch), lambda i: (i, 0)),
                  pl.BlockSpec((DEC_ROWS, heads, SSM_HEAD_DIM, SSM_D_STATE), lambda i: (i, 0, 0, 0))],
        out_specs=[pl.BlockSpec((DEC_ROWS, width), lambda i: (i, 0)),
                   pl.BlockSpec((DEC_ROWS, heads, SSM_HEAD_DIM, SSM_D_STATE), lambda i: (i, 0, 0, 0)),
                   pl.BlockSpec((DEC_ROWS, (SSM_CONV - 1) * ch), lambda i: (i, 0))],
        out_shape=[jax.ShapeDtypeStruct((m, width), F32),
                   jax.ShapeDtypeStruct(s0.shape, F32),
                   jax.ShapeDtypeStruct(conv0.shape, F32)],
        scratch_shapes=[pltpu.VMEM((DEC_ROWS, width), F32)],
        compiler_params=_cparams("parallel"),
        name="ssd_dec",
    )(p, dtx, p, sw["cw"], sw["cb"], sw["dtbx"], sw["alx"], sw["dx"], sw["gn"], conv0, s0)


def _dec_q_kernel(q_ref, wk_ref, tc_ref, ta_ref, tb_ref, o_ref, *, kvl):
    q = q_ref[...]
    qr = _rope128(q[:, MLA_NOPE:], tc_ref[...], ta_ref[...], tb_ref[...])
    o_ref[:, :kvl] = _dot_nt(q[:, :MLA_NOPE].astype(BF16), wk_ref[...])
    o_ref[:, kvl:] = qr


def _dec_q(q, wk, tabs, heads):
    m = q.shape[0]
    kvl = wk.shape[0]
    tab_spec = pl.BlockSpec((m, LANES), lambda h: (0, 0))
    return pl.pallas_call(
        functools.partial(_dec_q_kernel, kvl=kvl),
        grid=(heads,),
        in_specs=[pl.BlockSpec((m, MLA_QPAD), lambda h: (0, h)),
                  pl.BlockSpec((kvl, MLA_NOPE), lambda h: (0, h)),
                  tab_spec, tab_spec, tab_spec],
        out_specs=pl.BlockSpec((m, kvl + LANES), lambda h: (0, h)),
        out_shape=jax.ShapeDtypeStruct((m, heads * (kvl + LANES)), F32),
        compiler_params=_cparams("parallel"),
        name="dec_q",
    )(q, wk, *tabs)


def _dec_attn_kernel(pt_ref, q_ref, lat_new_ref, kr_new_ref, *rest, pps, kvl, scale):
    lat_refs = rest[:pps]
    kr_refs = rest[pps:2 * pps]
    o_ref, m_ref, l_ref, acc_ref = rest[2 * pps:]
    b = pl.program_id(0)
    g = pl.program_id(1)
    q = q_ref[0]
    q_lat = q[:, :kvl]
    q_rope = q[:, kvl:kvl + MLA_ROPE]

    @pl.when(g == 0)
    def _():
        r = b % 8
        lat_new = lat_new_ref[pl.ds(r, 1), :]
        kr_new = kr_new_ref[pl.ds(r, 1), :]
        s_new = (jnp.sum(q_lat * lat_new, axis=1, keepdims=True)
                 + jnp.sum(q_rope * kr_new, axis=1, keepdims=True)) * scale
        m_ref[...] = s_new
        l_ref[...] = jnp.ones(l_ref.shape, F32)
        acc_ref[...] = jnp.broadcast_to(lat_new, acc_ref.shape)

    qlb = q_lat.astype(BF16)
    qrb = q_rope.astype(BF16)
    lats, ss = [], []
    for j in range(pps):
        lat = lat_refs[j][...].astype(BF16)
        kr = kr_refs[j][...].astype(BF16)
        lats.append(lat)
        ss.append((_dot_nt(qlb, lat) + _dot_nt(qrb, kr)) * scale)
    m_prev = m_ref[...]
    m_new = m_prev
    for s in ss:
        m_new = jnp.maximum(m_new, jnp.max(s, axis=1, keepdims=True))
    corr = jnp.exp(m_prev - m_new)
    l_new = l_ref[...] * corr
    acc = acc_ref[...] * corr
    for j in range(pps):
        pe = jnp.exp(ss[j] - m_new)
        l_new = l_new + jnp.sum(pe, axis=1, keepdims=True)
        acc = acc + jnp.dot(pe.astype(BF16), lats[j], preferred_element_type=F32)
    m_ref[...] = m_new
    l_ref[...] = l_new
    acc_ref[...] = acc

    @pl.when(g == pl.num_programs(1) - 1)
    def _():
        o_ref[0] = acc / l_new


def _dec_attn(page_table, q3, lat_new, kr_new, cache_lat, cache_kr, layer):
    m, heads, qw = q3.shape
    kvl = cache_lat.shape[-1]
    n_pages = page_table.shape[1]
    pps = min(DEC_PAGES_PER_STEP, n_pages)
    scale = (MLA_NOPE + MLA_ROPE) ** -0.5

    def page_spec(width, j):
        return pl.BlockSpec((None, None, PAGE_SIZE, width),
                            lambda b, g, pt: (layer, pt[b, g * pps + j], 0, 0))

    in_specs = [pl.BlockSpec((1, heads, qw), lambda b, g, pt: (b, 0, 0)),
                pl.BlockSpec((8, kvl), lambda b, g, pt: (b // 8, 0)),
                pl.BlockSpec((8, MLA_ROPE), lambda b, g, pt: (b // 8, 0))]
    in_specs += [page_spec(kvl, j) for j in range(pps)]
    in_specs += [page_spec(MLA_ROPE, j) for j in range(pps)]
    return pl.pallas_call(
        functools.partial(_dec_attn_kernel, pps=pps, kvl=kvl, scale=scale),
        grid_spec=pltpu.PrefetchScalarGridSpec(
            num_scalar_prefetch=1,
            grid=(m, n_pages // pps),
            in_specs=in_specs,
            out_specs=pl.BlockSpec((1, heads, kvl), lambda b, g, pt: (b, 0, 0)),
            scratch_shapes=[pltpu.VMEM((heads, 1), F32), pltpu.VMEM((heads, 1), F32),
                            pltpu.VMEM((heads, kvl), F32)]),
        out_shape=jax.ShapeDtypeStruct((m, heads, kvl), F32),
        compiler_params=_cparams("parallel", "arbitrary"),
        name="dec_attn",
    )(page_table, q3, lat_new, kr_new, *([cache_lat] * pps), *([cache_kr] * pps))


def _dec_ov_kernel(o_ref, wv_ref, gate_ref, out_ref):
    out_ref[...] = (jnp.dot(o_ref[...].astype(BF16), wv_ref[...], preferred_element_type=F32)
                    * _silu(gate_ref[...]))


def _dec_ov(o_lat2, wv, p, gate_blk0, heads):
    m = o_lat2.shape[0]
    kvl = wv.shape[0]
    return pl.pallas_call(
        _dec_ov_kernel,
        grid=(heads,),
        in_specs=[pl.BlockSpec((m, kvl), lambda h: (0, h)),
                  pl.BlockSpec((kvl, MLA_V), lambda h: (0, h)),
                  pl.BlockSpec((m, MLA_V), lambda h: (0, gate_blk0 + h))],
        out_specs=pl.BlockSpec((m, MLA_V), lambda h: (0, h)),
        out_shape=jax.ShapeDtypeStruct((m, heads * MLA_V), F32),
        compiler_params=_cparams("parallel"),
        name="dec_ov",
    )(o_lat2, wv, p)


def _post_kernel(y_ref, h_ref, g_ref, o_ref):
    y = y_ref[...]
    ms = jnp.mean(y * y, axis=-1, keepdims=True)
    o_ref[...] = h_ref[...] + y * lax.rsqrt(ms + EPS) * g_ref[...]


def _post(y, h, g, bm):
    m, d = y.shape
    return pl.pallas_call(
        _post_kernel,
        grid=(m // bm,),
        in_specs=[pl.BlockSpec((bm, d), lambda i: (i, 0)),
                  pl.BlockSpec((bm, d), lambda i: (i, 0)),
                  pl.BlockSpec((1, d), lambda i: (0, 0))],
        out_specs=pl.BlockSpec((bm, d), lambda i: (i, 0)),
        out_shape=jax.ShapeDtypeStruct((m, d), F32),
        compiler_params=_cparams("parallel"),
        name="post_norm",
    )(y, h, g)


def _layout(segments):
    off, offs = 0, {}
    for name, width, block in segments:
        off = -(-off // block) * block
        offs[name] = off
        off += width
    return offs, -(-off // LANES) * LANES


def _rope_tables(pos):
    half = MLA_ROPE // 2
    inv_freq = ROPE_THETA ** (-jnp.arange(half, dtype=F32) / half)
    ang = pos.astype(F32)[:, None] * inv_freq[None, :]
    cos, sin = jnp.cos(ang), jnp.sin(ang)
    zeros = jnp.zeros_like(cos)
    pad = jnp.zeros((pos.shape[0], LANES - MLA_ROPE), F32)
    tc = jnp.concatenate([cos, cos, pad], axis=1)
    ta = jnp.concatenate([-sin, zeros, pad], axis=1)
    tb = jnp.concatenate([zeros, sin, pad], axis=1)
    return tc, ta, tb


def _pick(n, prefs):
    for b in prefs:
        if n % b == 0:
            return b
    return n


def kernel(x_prompt, x_sample, cache_mla_latent, cache_mla_krope, state_gla, state_ssm, state_conv, page_table,
           norm_pre, norm_post, w_in, mla_q_norm, mla_w_qb, mla_kv_norm, mla_w_kb, mla_w_vb, gla_w_a2, gla_b_a,
           gla_norm, ssm_conv_w, ssm_conv_b, ssm_dt_bias, ssm_a_log, ssm_d, ssm_norm, w_out):
    depth = w_in.shape[0]
    bp, lp, d = x_prompt.shape
    bs, ls, _ = x_sample.shape
    assert ls == 1
    q_lora = mla_q_norm.shape[1]
    kvl = mla_kv_norm.shape[1]
    heads = mla_w_kb.shape[2]
    dv = gla_norm.shape[1]
    dk = gla_w_a2.shape[2] // GLA_HEADS
    s_heads = ssm_d.shape[1]
    s_width = s_heads * SSM_HEAD_DIM
    conv_ch = ssm_conv_w.shape[2]
    mla_w = heads * MLA_V
    gla_w = GLA_HEADS * dv
    assert dk == LANES and dv % LANES == 0 and kvl % LANES == 0

    in_sizes = (q_lora, kvl + MLA_ROPE, GLA_HEADS * dk, GLA_HEADS * dk, gla_w, gla_w_a2.shape[1], conv_ch, s_heads,
                mla_w, gla_w, s_width)
    names = ("q_a", "kv_a", "g_q", "g_k", "g_v", "g_a", "xbc", "dt", "gate_mla", "gate_gla", "z")
    src, acc = {}, 0
    for nm, sz in zip(names, in_sizes):
        src[nm] = (acc, sz)
        acc += sz
    segs = [("q_a", q_lora, q_lora), ("z", s_width, s_width), ("g_v", gla_w, gla_w), ("gate_gla", gla_w, gla_w),
            ("gate_mla", mla_w, MLA_V), ("xbc", conv_ch, conv_ch), ("ckv", kvl, kvl),
            ("g_q", GLA_HEADS * dk, GLA_HEADS * dk), ("g_k", GLA_HEADS * dk, GLA_HEADS * dk)]
    offs, n_main = _layout(segs)
    blk = {"q_a": offs["q_a"] // q_lora, "z": offs["z"] // s_width, "g_v": offs["g_v"] // dv,
           "gate_gla": offs["gate_gla"] // dv, "gate_mla": offs["gate_mla"] // MLA_V,
           "xbc": offs["xbc"] // conv_ch, "ckv": offs["ckv"] // kvl, "g_q": offs["g_q"] // dk,
           "g_k": offs["g_k"] // dk}

    def w_in_main(wl):
        cols = jnp.zeros((d, n_main), BF16)
        for nm, width, _ in segs:
            s0 = src["kv_a"][0] if nm == "ckv" else src[nm][0]
            cols = lax.dynamic_update_slice(cols, wl[:, s0:s0 + width].astype(BF16), (0, offs[nm]))
        return cols

    def w_in_small(wl):
        kr0 = src["kv_a"][0] + kvl
        parts = [wl[:, kr0:kr0 + MLA_ROPE], wl[:, src["g_a"][0]:src["g_a"][0] + src["g_a"][1]],
                 wl[:, src["dt"][0]:src["dt"][0] + s_heads]]
        assert src["g_a"][1] == SMALL_DT - SMALL_GA and SMALL_DT + s_heads <= SMALL_W
        used = MLA_ROPE + src["g_a"][1] + s_heads
        parts.append(jnp.zeros((d, SMALL_W - used), F32))
        return jnp.concatenate(parts, axis=1).astype(BF16)

    mp, ms_ = bp * lp, bs
    n_pages = page_table.shape[1]
    tabs_p = _rope_tables(jnp.arange(lp))
    tabs_s = _rope_tables(jnp.full((ms_,), n_pages * PAGE_SIZE))
    hp = x_prompt.reshape(mp, d)
    hs = x_sample.reshape(ms_, d)
    bn_main = _pick(n_main, (1024, 512, 256, 128))
    bn_out = _pick(d, (1024, 512, 256, 128))
    bm_p = _pick(mp, (1024, 512, 256))
    bm_row = _pick(mp, (256,))
    bm_kv = _pick(lp, (256, 128))
    conv0_p = jnp.zeros((bp, SSM_CONV - 1, conv_ch), F32)
    ssm0_p = jnp.zeros((bp, s_heads, SSM_HEAD_DIM, SSM_D_STATE), F32)

    rows_p, rows_s = [], []
    for l in range(depth):
        wl = w_in[l]
        w_main = w_in_main(wl)
        w_small = w_in_small(wl)
        w_qb = jnp.pad(mla_w_qb[l].reshape(q_lora, heads, MLA_NOPE + MLA_ROPE),
                       ((0, 0), (0, 0), (0, MLA_QPAD - MLA_NOPE - MLA_ROPE))).reshape(q_lora, heads * MLA_QPAD)
        w_qb = w_qb.astype(BF16)
        w_kb = mla_w_kb[l].reshape(kvl, heads * MLA_NOPE).astype(BF16)
        w_vb = mla_w_vb[l].reshape(kvl, heads * MLA_V).astype(BF16)
        w_o = w_out[l].astype(BF16)
        wa_pad = jnp.zeros((SMALL_W, GLA_HEADS * dk), F32)
        wa_pad = lax.dynamic_update_slice(wa_pad, gla_w_a2[l], (SMALL_GA, 0)).astype(BF16)
        ba = gla_b_a[l][None, :]
        gn_gla = gla_norm[l][None, :]
        sw = {"cw": ssm_conv_w[l], "cb": ssm_conv_b[l][None, :], "dtb": ssm_dt_bias[l][None, :],
              "dtbc": ssm_dt_bias[l][:, None], "al": ssm_a_log[l][None, :], "alc": ssm_a_log[l][:, None],
              "dx": jnp.repeat(ssm_d[l], SSM_HEAD_DIM)[None, :], "gn": ssm_norm[l][None, :],
              "dtbx": jnp.repeat(ssm_dt_bias[l], SSM_HEAD_DIM)[None, :],
              "alx": jnp.repeat(ssm_a_log[l], SSM_HEAD_DIM)[None, :]}
        g_pre = norm_pre[l][None, :]
        g_post = norm_post[l][None, :]
        g_q = mla_q_norm[l][None, :]
        g_kv = mla_kv_norm[l][None, :]

        u, small = _norm_small(hp, g_pre, w_small, bm_row)
        p = _matmul([u], w_main, [d], bm_p, bn_main)
        q = _norm_matmul(p, blk["q_a"], g_q, w_qb, _pick(mp, (512, 256)), _pick(heads * MLA_QPAD, (1024, 512, 256)))
        lat, kr, kcat, v = _mla_kv(p, blk["ckv"], small, g_kv, tabs_p, w_kb, w_vb, bm_kv, True)
        o_mla = _flash(q, kcat, v, p, blk["gate_mla"], tabs_p, bp, lp, heads)
        o_gla, gla_new = _gla_prompt(p, small, blk, wa_pad, ba, gn_gla, bp, lp, dk, dv)
        dtt = small[:, SMALL_DT:SMALL_DT + s_heads].reshape(bp, lp, s_heads).transpose(0, 2, 1)
        o_ssm, ssm_new, conv_new = _ssd_prompt(p, small, dtt, blk, sw, conv0_p, ssm0_p, bp, lp)
        y = _matmul([o_mla, o_gla, o_ssm], w_o, [mla_w, gla_w, s_width], bm_p, bn_out)
        hp = _post(y, hp, g_post, bm_row)
        rows_p.append((lat.reshape(bp, lp, kvl), kr.reshape(bp, lp, MLA_ROPE), gla_new, ssm_new, conv_new))

        u, small = _norm_small(hs, g_pre, w_small, ms_)
        p = _matmul([u], w_main, [d], ms_, bn_main)
        q = _norm_matmul(p, blk["q_a"], g_q, w_qb, ms_, _pick(heads * MLA_QPAD, (1024, 512, 256)))
        lat, kr = _mla_kv(p, blk["ckv"], small, g_kv, tabs_s, w_kb, w_vb, ms_, False)
        qcat = _dec_q(q, w_kb, tabs_s, heads).reshape(ms_, heads, kvl + LANES)
        o_lat = _dec_attn(page_table, qcat, lat, kr, cache_mla_latent, cache_mla_krope, l)
        o_mla = _dec_ov(o_lat.reshape(ms_, heads * kvl), w_vb, p, blk["gate_mla"], heads)
        o_gla, gla_new = _gla_dec(p, small, blk, wa_pad, ba, gn_gla, state_gla[l], dk, dv)
        dtx = jnp.repeat(small[:, SMALL_DT:SMALL_DT + s_heads], SSM_HEAD_DIM, axis=1)
        o_ssm, ssm_new, conv_new = _ssd_dec(p, dtx, blk, sw, state_conv[l].reshape(ms_, (SSM_CONV - 1) * conv_ch),
                                            state_ssm[l])
        y = _matmul([o_mla, o_gla, o_ssm], w_o, [mla_w, gla_w, s_width], ms_, bn_out)
        hs = _post(y, hs, g_post, ms_)
        rows_s.append((lat.reshape(bs, 1, kvl), kr.reshape(bs, 1, MLA_ROPE), gla_new, ssm_new,
                       conv_new.reshape(bs, SSM_CONV - 1, conv_ch)))

    outs = [hp.reshape(bp, lp, d), hs.reshape(bs, 1, d)]
    for rows in (rows_p, rows_s):
        for i in range(5):
            outs.append(jnp.stack([r[i] for r in rows]))
    return tuple(outs)
```

```python
import functools
import math

import jax
import jax.numpy as jnp
from jax import lax
from jax.experimental import pallas as pl
from jax.experimental.pallas import tpu as pltpu

F32 = jnp.float32
BF16 = jnp.bfloat16

EPS = 1e-6
MLA_V = 128
MLA_NOPE = 128
MLA_ROPE = 64
MLA_QPAD = 256
ROPE_THETA = 10000.0
GLA_HEADS = 4
GLA_GATE_NORM = 16.0
GLA_CHUNK = 32
SSM_HEAD_DIM = 64
SSM_GROUPS = 2
SSM_D_STATE = 128
SSM_CONV = 4
PAGE_SIZE = 128
LANES = 128
SMALL_W = 128
SMALL_KROPE = 0
SMALL_GA = 64
SMALL_DT = 80

GLA_TILE = 256
SSD_TILE = 256
FLASH_BQ = 512
DEC_PAGES_PER_STEP = 16
DEC_ROWS = 8
VMEM_LIMIT = 52 * 1024 * 1024


def _cparams(*sem):
    return pltpu.CompilerParams(dimension_semantics=sem, vmem_limit_bytes=VMEM_LIMIT)


def _silu(x):
    return x * (1.0 / (1.0 + jnp.exp(-x)))


def _softplus(x):
    return jnp.maximum(x, 0.0) + jnp.log1p(jnp.exp(-jnp.abs(x)))


def _split3(x):
    a = x.astype(BF16)
    r = x - a.astype(F32)
    b = r.astype(BF16)
    c = (r - b.astype(F32)).astype(BF16)
    return a, b, c


def _tri_rows(t, block):
    i = lax.broadcasted_iota(jnp.int32, (t, t), 0)
    j = lax.broadcasted_iota(jnp.int32, (t, t), 1)
    return ((j <= i) & ((i // block) == (j // block))).astype(BF16)


def _tri_cols(t):
    i = lax.broadcasted_iota(jnp.int32, (t, t), 0)
    j = lax.broadcasted_iota(jnp.int32, (t, t), 1)
    return (i <= j).astype(BF16)


def _cumsum_rows(x, tri):
    a, b, c = _split3(x)
    d = functools.partial(jnp.dot, preferred_element_type=F32)
    return d(tri, a) + d(tri, b) + d(tri, c)


def _cumsum_lanes(x, tri_t):
    a, b, c = _split3(x)
    d = functools.partial(jnp.dot, preferred_element_type=F32)
    return d(a, tri_t) + d(b, tri_t) + d(c, tri_t)


def _dot_nt(a, b):
    return lax.dot_general(a, b, (((1,), (1,)), ((), ())), preferred_element_type=F32)


def _dot_tn(a, b):
    return lax.dot_general(a, b, (((0,), (0,)), ((), ())), preferred_element_type=F32)


def _col_bcast(row):
    return jnp.broadcast_to(row, (LANES, LANES)).T


def _rope128(x, tc, ta, tb):
    return x * tc + pltpu.roll(x, 96, axis=1) * ta + pltpu.roll(x, 32, axis=1) * tb


def _norm_small_kernel(x_ref, g_ref, ws_ref, u_ref, s_ref):
    x = x_ref[...]
    ms = jnp.mean(x * x, axis=-1, keepdims=True)
    u = (x * lax.rsqrt(ms + EPS) * g_ref[...]).astype(BF16)
    u_ref[...] = u
    s_ref[...] = jnp.dot(u, ws_ref[...], preferred_element_type=F32)


def _norm_small(x, g, w_small, bm):
    m, d = x.shape
    return pl.pallas_call(
        _norm_small_kernel,
        grid=(m // bm,),
        in_specs=[pl.BlockSpec((bm, d), lambda i: (i, 0)),
                  pl.BlockSpec((1, d), lambda i: (0, 0)),
                  pl.BlockSpec((d, SMALL_W), lambda i: (0, 0))],
        out_specs=[pl.BlockSpec((bm, d), lambda i: (i, 0)),
                   pl.BlockSpec((bm, SMALL_W), lambda i: (i, 0))],
        out_shape=[jax.ShapeDtypeStruct((m, d), BF16), jax.ShapeDtypeStruct((m, SMALL_W), F32)],
        compiler_params=_cparams("parallel"),
        name="norm_small",
    )(x, g, w_small)


def _mm_kernel(*refs, n):
    o_ref = refs[2 * n]
    acc = None
    for i in range(n):
        p = jnp.dot(refs[i][...].astype(BF16), refs[n + i][...], preferred_element_type=F32)
        acc = p if acc is None else acc + p
    o_ref[...] = acc.astype(o_ref.dtype)


def _matmul(lhs_list, w, k_sizes, bm, bn, out_dtype=F32):
    n = len(lhs_list)
    m = lhs_list[0].shape[0]
    ncols = w.shape[1]
    in_specs, k0 = [], 0
    for a in lhs_list:
        in_specs.append(pl.BlockSpec((bm, a.shape[1]), lambda i, j: (i, 0)))
    for ks in k_sizes:
        assert k0 % ks == 0
        in_specs.append(pl.BlockSpec((ks, bn), functools.partial(lambda i, j, kb: (kb, j), kb=k0 // ks)))
        k0 += ks
    return pl.pallas_call(
        functools.partial(_mm_kernel, n=n),
        grid=(m // bm, ncols // bn),
        in_specs=in_specs,
        out_specs=pl.BlockSpec((bm, bn), lambda i, j: (i, j)),
        out_shape=jax.ShapeDtypeStruct((m, ncols), out_dtype),
        compiler_params=_cparams("parallel", "parallel"),
        name="matmul%d" % n,
    )(*lhs_list, *([w] * n))


def _norm_mm_kernel(x_ref, g_ref, w_ref, o_ref):
    x = x_ref[...]
    ms = jnp.mean(x * x, axis=-1, keepdims=True)
    xn = (x * lax.rsqrt(ms + EPS) * g_ref[...]).astype(BF16)
    o_ref[...] = jnp.dot(xn, w_ref[...], preferred_element_type=F32)


def _norm_matmul(p, col_blk, g, w, bm, bn):
    m = p.shape[0]
    k, ncols = w.shape
    return pl.pallas_call(
        _norm_mm_kernel,
        grid=(m // bm, ncols // bn),
        in_specs=[pl.BlockSpec((bm, k), lambda i, j: (i, col_blk)),
                  pl.BlockSpec((1, k), lambda i, j: (0, 0)),
                  pl.BlockSpec((k, bn), lambda i, j: (0, j))],
        out_specs=pl.BlockSpec((bm, bn), lambda i, j: (i, j)),
        out_shape=jax.ShapeDtypeStruct((m, ncols), F32),
        compiler_params=_cparams("parallel", "parallel"),
        name="q_proj",
    )(p, g, w)


def _mla_kv_kernel(ckv_ref, small_ref, g_ref, tc_ref, ta_ref, tb_ref, *rest, expand, heads):
    if expand:
        wk_ref, wv_ref, lat_ref, kr_ref, kcat_ref, v_ref = rest
    else:
        lat_ref, kr_ref = rest
    x = ckv_ref[...]
    ms = jnp.mean(x * x, axis=-1, keepdims=True)
    c = x * lax.rsqrt(ms + EPS) * g_ref[...]
    lat_ref[...] = c
    kr = _rope128(small_ref[...], tc_ref[...], ta_ref[...], tb_ref[...])
    kr_ref[...] = kr[:, :MLA_ROPE]
    if expand:
        cb = c.astype(BF16)
        kn = jnp.dot(cb, wk_ref[...], preferred_element_type=F32).astype(BF16)
        v_ref[...] = jnp.dot(cb, wv_ref[...], preferred_element_type=F32).astype(BF16)
        krb = kr.astype(BF16)
        for h in range(heads):
            kcat_ref[:, h * MLA_QPAD:h * MLA_QPAD + MLA_NOPE] = kn[:, h * MLA_NOPE:(h + 1) * MLA_NOPE]
            kcat_ref[:, h * MLA_QPAD + MLA_NOPE:(h + 1) * MLA_QPAD] = krb


def _mla_kv(p, ckv_blk, small, g, tabs, wk, wv, bm, expand):
    m = p.shape[0]
    kvl = g.shape[1]
    nt = tabs[0].shape[0] // bm
    heads = wk.shape[1] // MLA_NOPE
    tab_spec = pl.BlockSpec((bm, LANES), lambda i: (i % nt, 0))
    in_specs = [pl.BlockSpec((bm, kvl), lambda i: (i, ckv_blk)),
                pl.BlockSpec((bm, SMALL_W), lambda i: (i, 0)),
                pl.BlockSpec((1, kvl), lambda i: (0, 0)),
                tab_spec, tab_spec, tab_spec]
    out_specs = [pl.BlockSpec((bm, kvl), lambda i: (i, 0)),
                 pl.BlockSpec((bm, MLA_ROPE), lambda i: (i, 0))]
    out_shape = [jax.ShapeDtypeStruct((m, kvl), F32), jax.ShapeDtypeStruct((m, MLA_ROPE), F32)]
    args = [p, small, g, *tabs]
    if expand:
        in_specs += [pl.BlockSpec(wk.shape, lambda i: (0, 0)), pl.BlockSpec(wv.shape, lambda i: (0, 0))]
        out_specs += [pl.BlockSpec((bm, heads * MLA_QPAD), lambda i: (i, 0)),
                      pl.BlockSpec((bm, heads * MLA_V), lambda i: (i, 0))]
        out_shape += [jax.ShapeDtypeStruct((m, heads * MLA_QPAD), BF16),
                      jax.ShapeDtypeStruct((m, heads * MLA_V), BF16)]
        args += [wk, wv]
    return pl.pallas_call(
        functools.partial(_mla_kv_kernel, expand=expand, heads=heads),
        grid=(m // bm,),
        in_specs=in_specs, out_specs=out_specs, out_shape=out_shape,
        compiler_params=_cparams("parallel"),
        name="mla_kv",
    )(*args)


def _flash_kernel(q_ref, k_ref, v_ref, gate_ref, tc_ref, ta_ref, tb_ref, o_ref, qs_ref, *, seq, bq, scale):
    q = q_ref[...]
    qr = _rope128(q[:, MLA_NOPE:], tc_ref[...], ta_ref[...], tb_ref[...])
    qs_ref[:, :MLA_NOPE] = (q[:, :MLA_NOPE] * scale).astype(BF16)
    qs_ref[:, MLA_NOPE:] = (qr * scale).astype(BF16)
    ii = lax.broadcasted_iota(jnp.int32, (bq, bq), 0)
    jj = lax.broadcasted_iota(jnp.int32, (bq, bq), 1)
    for i in range(seq // bq):
        rows = slice(i * bq, (i + 1) * bq)
        qi = qs_ref[rows, :]
        s_d = jnp.where(jj <= ii, _dot_nt(qi, k_ref[rows, :]), -jnp.inf)
        m = jnp.max(s_d, axis=1, keepdims=True)
        if i > 0:
            s_a = _dot_nt(qi, k_ref[0:i * bq, :])
            m = jnp.maximum(m, jnp.max(s_a, axis=1, keepdims=True))
        p_d = jnp.exp(s_d - m)
        l = jnp.sum(p_d, axis=1, keepdims=True)
        o = jnp.dot(p_d.astype(BF16), v_ref[rows, :], preferred_element_type=F32)
        if i > 0:
            p_a = jnp.exp(s_a - m)
            l = l + jnp.sum(p_a, axis=1, keepdims=True)
            o = o + jnp.dot(p_a.astype(BF16), v_ref[0:i * bq, :], preferred_element_type=F32)
        o_ref[rows, :] = (o * (1.0 / l) * _silu(gate_ref[rows, :])).astype(o_ref.dtype)


def _flash(q, kcat, v, p, gate_blk0, tabs, batch, seq, heads):
    m = q.shape[0]
    bq = min(FLASH_BQ, seq)
    scale = (MLA_NOPE + MLA_ROPE) ** -0.5
    tab_spec = pl.BlockSpec((seq, LANES), lambda b, h: (0, 0))
    return pl.pallas_call(
        functools.partial(_flash_kernel, seq=seq, bq=bq, scale=scale),
        grid=(batch, heads),
        in_specs=[pl.BlockSpec((seq, MLA_QPAD), lambda b, h: (b, h)),
                  pl.BlockSpec((seq, MLA_QPAD), lambda b, h: (b, h)),
                  pl.BlockSpec((seq, MLA_V), lambda b, h: (b, h)),
                  pl.BlockSpec((seq, MLA_V), lambda b, h: (b, gate_blk0 + h)),
                  tab_spec, tab_spec, tab_spec],
        out_specs=pl.BlockSpec((seq, MLA_V), lambda b, h: (b, h)),
        out_shape=jax.ShapeDtypeStruct((m, heads * MLA_V), BF16),
        scratch_shapes=[pltpu.VMEM((seq, MLA_QPAD), BF16)],
        compiler_params=_cparams("parallel", "parallel"),
        name="flash",
    )(q, kcat, v, p, *tabs)


def _log_decay(small, wa_ref, ba_ref):
    x = jnp.dot(small.astype(BF16), wa_ref[...], preferred_element_type=F32) + ba_ref[...]
    return -_softplus(-x) * (1.0 / GLA_GATE_NORM)


def _gla_kernel(q_ref, k_ref, v_ref, small_ref, gate_ref, wa_ref, ba_ref, gn_ref, o_ref, st_ref, s_ref,
                *, tile, dk, dv):
    c = pl.program_id(2)

    @pl.when(c == 0)
    def _():
        s_ref[...] = jnp.zeros(s_ref.shape, F32)

    la = _log_decay(small_ref[...], wa_ref, ba_ref)
    bcs = _cumsum_rows(la, _tri_rows(tile, GLA_CHUNK))
    ii = lax.broadcasted_iota(jnp.int32, (GLA_CHUNK, GLA_CHUNK), 0)
    jj = lax.broadcasted_iota(jnp.int32, (GLA_CHUNK, GLA_CHUNK), 1)
    scale = dk ** -0.5
    outs = []
    for i in range(tile // GLA_CHUNK):
        sl = slice(i * GLA_CHUNK, (i + 1) * GLA_CHUNK)
        bi = bcs[sl]
        ki = k_ref[sl, :]
        vi = v_ref[sl, :].astype(BF16)
        qd = (q_ref[sl, :] * scale * jnp.exp(bi)).astype(BF16)
        kd = (ki * jnp.exp(-bi)).astype(BF16)
        att = jnp.where(jj <= ii, _dot_nt(qd, kd), 0.0).astype(BF16)
        s_old = s_ref[...]
        o = (jnp.dot(qd, s_old.astype(BF16), preferred_element_type=F32)
             + jnp.dot(att, vi, preferred_element_type=F32))
        outs.append(o)
        bl = bi[GLA_CHUNK - 1:GLA_CHUNK, :]
        kv = _dot_tn((ki * jnp.exp(bl - bi)).astype(BF16), vi)
        dcol = _col_bcast(jnp.exp(bl))
        for j in range(dv // LANES):
            ls = slice(j * LANES, (j + 1) * LANES)
            s_ref[:, ls] = dcol * s_old[:, ls] + kv[:, ls]
    o = jnp.concatenate(outs, axis=0)
    ms = jnp.mean(o * o, axis=-1, keepdims=True)
    o = o * lax.rsqrt(ms + EPS) * gn_ref[...]
    o_ref[...] = (o * _silu(gate_ref[...])).astype(o_ref.dtype)

    @pl.when(c == pl.num_programs(2) - 1)
    def _():
        st_ref[0, 0] = s_ref[...]


def _gla_prompt(p, small, blk, wa_pad, ba, gn, batch, seq, dk, dv):
    m = p.shape[0]
    tile = min(GLA_TILE, seq)
    nb = seq // tile
    row = lambda b, h, c: b * nb + c
    return pl.pallas_call(
        functools.partial(_gla_kernel, tile=tile, dk=dk, dv=dv),
        grid=(batch, GLA_HEADS, nb),
        in_specs=[pl.BlockSpec((tile, dk), lambda b, h, c: (row(b, h, c), blk["g_q"] + h)),
                  pl.BlockSpec((tile, dk), lambda b, h, c: (row(b, h, c), blk["g_k"] + h)),
                  pl.BlockSpec((tile, dv), lambda b, h, c: (row(b, h, c), blk["g_v"] + h)),
                  pl.BlockSpec((tile, SMALL_W), lambda b, h, c: (row(b, h, c), 0)),
                  pl.BlockSpec((tile, dv), lambda b, h, c: (row(b, h, c), blk["gate_gla"] + h)),
                  pl.BlockSpec((SMALL_W, dk), lambda b, h, c: (0, h)),
                  pl.BlockSpec((1, dk), lambda b, h, c: (0, h)),
                  pl.BlockSpec((1, dv), lambda b, h, c: (0, 0))],
        out_specs=[pl.BlockSpec((tile, dv), lambda b, h, c: (row(b, h, c), h)),
                   pl.BlockSpec((1, 1, dk, dv), lambda b, h, c: (b, h, 0, 0))],
        out_shape=[jax.ShapeDtypeStruct((m, GLA_HEADS * dv), BF16),
                   jax.ShapeDtypeStruct((batch, GLA_HEADS, dk, dv), F32)],
        scratch_shapes=[pltpu.VMEM((dk, dv), F32)],
        compiler_params=_cparams("parallel", "parallel", "arbitrary"),
        name="gla_prompt",
    )(p, p, p, small, p, wa_pad, ba, gn)


def _gla_dec_kernel(q_ref, k_ref, v_ref, small_ref, gate_ref, wa_ref, ba_ref, gn_ref, s0_ref, *rest, dk, dv):
    o_ref, st_ref, ob_ref = rest[-3:]
    la = _log_decay(small_ref[...], wa_ref, ba_ref)
    a = jnp.exp(la)
    q = q_ref[...] * (dk ** -0.5)
    k = k_ref[...]
    v = v_ref[...]
    for r in range(DEC_ROWS):
        for h in range(GLA_HEADS):
            ks = slice(h * dk, (h + 1) * dk)
            acol = _col_bcast(a[r:r + 1, ks])
            kcol = _col_bcast(k[r:r + 1, ks])
            qcol = _col_bcast(q[r:r + 1, ks])
            for j in range(dv // LANES):
                ls = slice(j * LANES, (j + 1) * LANES)
                vrow = v[r:r + 1, h * dv + j * LANES:h * dv + (j + 1) * LANES]
                s_new = acol * s0_ref[r, h, :, ls] + kcol * vrow
                st_ref[r, h, :, ls] = s_new
                ob_ref[r:r + 1, h * dv + j * LANES:h * dv + (j + 1) * LANES] = jnp.sum(
                    qcol * s_new, axis=0, keepdims=True)
    for h in range(GLA_HEADS):
        vs = slice(h * dv, (h + 1) * dv)
        o = ob_ref[:, vs]
        ms = jnp.mean(o * o, axis=-1, keepdims=True)
        o_ref[:, vs] = o * lax.rsqrt(ms + EPS) * gn_ref[...] * _silu(gate_ref[:, vs])


def _gla_dec(p, small, blk, wa_pad, ba, gn, states, prev_new, layer, dk, dv):
    m = p.shape[0]
    hk, hv = GLA_HEADS * dk, GLA_HEADS * dv
    st_spec = pl.BlockSpec((None, DEC_ROWS, GLA_HEADS, dk, dv), lambda i: (layer, i, 0, 0, 0))
    in_specs = [pl.BlockSpec((DEC_ROWS, hk), lambda i: (i, blk["g_q"] * dk // hk)),
                pl.BlockSpec((DEC_ROWS, hk), lambda i: (i, blk["g_k"] * dk // hk)),
                pl.BlockSpec((DEC_ROWS, hv), lambda i: (i, blk["g_v"] * dv // hv)),
                pl.BlockSpec((DEC_ROWS, SMALL_W), lambda i: (i, 0)),
                pl.BlockSpec((DEC_ROWS, hv), lambda i: (i, blk["gate_gla"] * dv // hv)),
                pl.BlockSpec((SMALL_W, hk), lambda i: (0, 0)),
                pl.BlockSpec((1, hk), lambda i: (0, 0)),
                pl.BlockSpec((1, dv), lambda i: (0, 0)),
                st_spec]
    args = [p, p, p, small, p, wa_pad, ba, gn, states]
    aliases = {}
    if prev_new is not None:
        in_specs.append(pl.BlockSpec(memory_space=pl.ANY))
        aliases = {len(args): 1}
        args.append(prev_new)
    return pl.pallas_call(
        functools.partial(_gla_dec_kernel, dk=dk, dv=dv),
        grid=(m // DEC_ROWS,),
        in_specs=in_specs,
        out_specs=[pl.BlockSpec((DEC_ROWS, hv), lambda i: (i, 0)), st_spec],
        out_shape=[jax.ShapeDtypeStruct((m, hv), F32),
                   jax.ShapeDtypeStruct(states.shape, F32)],
        scratch_shapes=[pltpu.VMEM((DEC_ROWS, hv), F32)],
        input_output_aliases=aliases,
        compiler_params=_cparams("parallel"),
        name="gla_dec",
    )(*args)


def _group_norm_gate(y, z, gn, groups):
    w = y.shape[1] // groups
    yz = y * _silu(z)
    outs = []
    for g in range(groups):
        t = yz[:, g * w:(g + 1) * w]
        ms = jnp.mean(t * t, axis=-1, keepdims=True)
        outs.append(t * lax.rsqrt(ms + EPS) * gn[:, g * w:(g + 1) * w])
    return outs


def _ssd_kernel(xbc_ref, small_ref, dtt_ref, z_ref, cw_ref, cb_ref, dtb_ref, dtbc_ref, al_ref, alc_ref, dx_ref,
                gn_ref, conv0_ref, s0_ref, o_ref, st_ref, cv_ref, s_ref, up_ref, y_ref, *, tile, heads, width):
    c = pl.program_id(1)
    taps = SSM_CONV
    base = 8

    @pl.when(c == 0)
    def _():
        s_ref[...] = s0_ref[0]
        up_ref[base - (taps - 1):base, :] = conv0_ref[0]

    up_ref[base:base + tile, :] = xbc_ref[...]
    conv = cb_ref[...]
    for j in range(taps):
        conv = conv + cw_ref[j:j + 1, :] * up_ref[base - (taps - 1) + j:base - (taps - 1) + j + tile, :]
    tail = up_ref[base + tile - (taps - 1):base + tile, :]
    up_ref[base - (taps - 1):base, :] = tail

    @pl.when(c == pl.num_programs(1) - 1)
    def _():
        cv_ref[0] = tail

    xbc = _silu(conv)
    ns = SSM_D_STATE
    xs = xbc[:, :width]
    bm = xbc[:, width:width + SSM_GROUPS * ns].astype(BF16)
    cm = xbc[:, width + SSM_GROUPS * ns:].astype(BF16)
    dt = _softplus(small_ref[:, SMALL_DT:SMALL_DT + heads] + dtb_ref[...])
    a = -jnp.exp(al_ref[...]) * dt
    tri = _tri_rows(tile, tile)
    acs = _cumsum_rows(a, tri)
    dtt = _softplus(dtt_ref[0] + dtbc_ref[...])
    acs_t = _cumsum_lanes(-jnp.exp(alc_ref[...]) * dtt, _tri_cols(tile))
    a_last = acs[tile - 1:tile, :]
    e_cs = jnp.exp(acs)
    dec = jnp.exp(a_last - acs)
    e_last = jnp.exp(a_last)
    ii = lax.broadcasted_iota(jnp.int32, (tile, tile), 0)
    jj = lax.broadcasted_iota(jnp.int32, (tile, tile), 1)
    low = jj <= ii
    hp = SSM_HEAD_DIM
    per_group = heads // SSM_GROUPS
    scores = [_dot_nt(cm[:, g * ns:(g + 1) * ns], bm[:, g * ns:(g + 1) * ns]) for g in range(SSM_GROUPS)]
    for h in range(heads):
        g = h // per_group
        col = jnp.broadcast_to(acs[:, h:h + 1], (tile, tile))
        row = jnp.broadcast_to(acs_t[h:h + 1, :], (tile, tile))
        lm = jnp.where(low, jnp.exp(jnp.where(low, col - row, 0.0)), 0.0)
        xh = xs[:, h * hp:(h + 1) * hp]
        xdt = xh * jnp.broadcast_to(dt[:, h:h + 1], (tile, hp))
        y = jnp.dot((scores[g] * lm).astype(BF16), xdt.astype(BF16), preferred_element_type=F32)
        sh = s_ref[h]
        y = y + (_dot_nt(cm[:, g * ns:(g + 1) * ns], sh.astype(BF16))
                 * jnp.broadcast_to(e_cs[:, h:h + 1], (tile, hp)))
        st = _dot_tn((xdt * jnp.broadcast_to(dec[:, h:h + 1], (tile, hp))).astype(BF16),
                     bm[:, g * ns:(g + 1) * ns])
        s_ref[h] = jnp.broadcast_to(e_last[:, h:h + 1], (hp, ns)) * sh + st
        y_ref[:, h * hp:(h + 1) * hp] = y + dx_ref[:, h * hp:(h + 1) * hp] * xh
    outs = _group_norm_gate(y_ref[...], z_ref[...], gn_ref[...], SSM_GROUPS)
    gw = width // SSM_GROUPS
    for g in range(SSM_GROUPS):
        o_ref[:, g * gw:(g + 1) * gw] = outs[g].astype(o_ref.dtype)

    @pl.when(c == pl.num_programs(1) - 1)
    def _():
        st_ref[0] = s_ref[...]


def _ssd_prompt(p, small, dtt, blk, sw, conv0, s0, batch, seq):
    m = p.shape[0]
    heads = sw["dtb"].shape[1]
    width = heads * SSM_HEAD_DIM
    ch = sw["cw"].shape[1]
    tile = min(SSD_TILE, seq)
    nb = seq // tile
    full = lambda a: pl.BlockSpec(a.shape, lambda b, c: (0,) * a.ndim)
    return pl.pallas_call(
        functools.partial(_ssd_kernel, tile=tile, heads=heads, width=width),
        grid=(batch, nb),
        in_specs=[pl.BlockSpec((tile, ch), lambda b, c: (b * nb + c, blk["xbc"])),
                  pl.BlockSpec((tile, SMALL_W), lambda b, c: (b * nb + c, 0)),
                  pl.BlockSpec((1, heads, tile), lambda b, c: (b, 0, c)),
                  pl.BlockSpec((tile, width), lambda b, c: (b * nb + c, blk["z"])),
                  full(sw["cw"]), full(sw["cb"]), full(sw["dtb"]), full(sw["dtbc"]), full(sw["al"]),
                  full(sw["alc"]), full(sw["dx"]), full(sw["gn"]),
                  pl.BlockSpec((1, SSM_CONV - 1, ch), lambda b, c: (b, 0, 0)),
                  pl.BlockSpec((1, heads, SSM_HEAD_DIM, SSM_D_STATE), lambda b, c: (b, 0, 0, 0))],
        out_specs=[pl.BlockSpec((tile, width), lambda b, c: (b * nb + c, 0)),
                   pl.BlockSpec((1, heads, SSM_HEAD_DIM, SSM_D_STATE), lambda b, c: (b, 0, 0, 0)),
                   pl.BlockSpec((1, SSM_CONV - 1, ch), lambda b, c: (b, 0, 0))],
        out_shape=[jax.ShapeDtypeStruct((m, width), BF16),
                   jax.ShapeDtypeStruct(s0.shape, F32),
                   jax.ShapeDtypeStruct(conv0.shape, F32)],
        scratch_shapes=[pltpu.VMEM((heads, SSM_HEAD_DIM, SSM_D_STATE), F32),
                        pltpu.VMEM((8 + tile, ch), F32),
                        pltpu.VMEM((tile, width), F32)],
        compiler_params=_cparams("parallel", "arbitrary"),
        name="ssd_prompt",
    )(p, small, dtt, p, sw["cw"], sw["cb"], sw["dtb"], sw["dtbc"], sw["al"], sw["alc"], sw["dx"], sw["gn"],
      conv0, s0)


def _ssd_dec_kernel(xbc_ref, dtx_ref, z_ref, cw_ref, cb_ref, dtbx_ref, alx_ref, dx_ref, gn_ref, conv0_ref, s0_ref,
                    *rest, heads, width):
    o_ref, st_ref, cv_ref, y_ref = rest[-4:]
    taps = SSM_CONV
    ns = SSM_D_STATE
    xr = xbc_ref[...]
    conv = cb_ref[...] + cw_ref[taps - 1:taps, :] * xr
    for j in range(taps - 1):
        conv = conv + cw_ref[j:j + 1, :] * conv0_ref[j]
    for j in range(taps - 2):
        cv_ref[j] = conv0_ref[j + 1]
    cv_ref[taps - 2] = xr
    xbc = _silu(conv)
    xs = xbc[:, :width]
    bm = xbc[:, width:width + SSM_GROUPS * ns]
    cm = xbc[:, width + SSM_GROUPS * ns:].astype(BF16)
    dt = _softplus(dtx_ref[...] + dtbx_ref[...])
    da = jnp.exp(-jnp.exp(alx_ref[...]) * dt)
    xdt = xs * dt
    per_group = width // SSM_GROUPS
    for r in range(DEC_ROWS):
        for cidx in range(width // LANES):
            ls = slice(cidx * LANES, (cidx + 1) * LANES)
            g = (cidx * LANES) // per_group
            hs = cidx * (LANES // SSM_HEAD_DIM)
            dcol = _col_bcast(da[r:r + 1, ls])
            xcol = _col_bcast(xdt[r:r + 1, ls])
            parts = []
            for t in range(LANES // SSM_HEAD_DIM):
                sl = slice(t * SSM_HEAD_DIM, (t + 1) * SSM_HEAD_DIM)
                s_new = dcol[sl, :] * s0_ref[r, hs + t] + xcol[sl, :] * bm[r:r + 1, g * ns:(g + 1) * ns]
                st_ref[r, hs + t] = s_new
                parts.append(s_new)
            s_cat = jnp.concatenate(parts, axis=0).astype(BF16)
            crow = jnp.broadcast_to(cm[r:r + 1, g * ns:(g + 1) * ns], (8, ns))
            y_ref[r:r + 1, ls] = _dot_nt(crow, s_cat)[0:1, :]
    y = y_ref[...] + dx_ref[...] * xs
    outs = _group_norm_gate(y, z_ref[...], gn_ref[...], SSM_GROUPS)
    gw = width // SSM_GROUPS
    for g in range(SSM_GROUPS):
        o_ref[:, g * gw:(g + 1) * gw] = outs[g]


def _ssd_dec(p, dtx, blk, sw, conv_t, states, prev_new, layer):
    m = p.shape[0]
    heads = sw["dtb"].shape[1]
    width = heads * SSM_HEAD_DIM
    ch = sw["cw"].shape[1]
    full = lambda a: pl.BlockSpec(a.shape, lambda i: (0,) * a.ndim)
    st_spec = pl.BlockSpec((None, DEC_ROWS, heads, SSM_HEAD_DIM, SSM_D_STATE), lambda i: (layer, i, 0, 0, 0))
    cv_spec = pl.BlockSpec((None, SSM_CONV - 1, DEC_ROWS, ch), lambda i: (layer, 0, i, 0))
    in_specs = [pl.BlockSpec((DEC_ROWS, ch), lambda i: (i, blk["xbc"])),
                pl.BlockSpec((DEC_ROWS, width), lambda i: (i, 0)),
                pl.BlockSpec((DEC_ROWS, width), lambda i: (i, blk["z"])),
                full(sw["cw"]), full(sw["cb"]), full(sw["dtbx"]), full(sw["alx"]), full(sw["dx"]), full(sw["gn"]),
                cv_spec, st_spec]
    args = [p, dtx, p, sw["cw"], sw["cb"], sw["dtbx"], sw["alx"], sw["dx"], sw["gn"], conv_t, states]
    aliases = {}
    if prev_new is not None:
        in_specs += [pl.BlockSpec(memory_space=pl.ANY)] * 2
        aliases = {len(args): 1, len(args) + 1: 2}
        args += list(prev_new)
    return pl.pallas_call(
        functools.partial(_ssd_dec_kernel, heads=heads, width=width),
        grid=(m // DEC_ROWS,),
        in_specs=in_specs,
        out_specs=[pl.BlockSpec((DEC_ROWS, width), lambda i: (i, 0)), st_spec, cv_spec],
        out_shape=[jax.ShapeDtypeStruct((m, width), F32),
                   jax.ShapeDtypeStruct(states.shape, F32),
                   jax.ShapeDtypeStruct(conv_t.shape, F32)],
        scratch_shapes=[pltpu.VMEM((DEC_ROWS, width), F32)],
        input_output_aliases=aliases,
        compiler_params=_cparams("parallel"),
        name="ssd_dec",
    )(*args)


def _dec_q_kernel(q_ref, wk_ref, tc_ref, ta_ref, tb_ref, o_ref, *, kvl):
    q = q_ref[...]
    qr = _rope128(q[:, MLA_NOPE:], tc_ref[...], ta_ref[...], tb_ref[...])
    o_ref[:, :kvl] = _dot_nt(q[:, :MLA_NOPE].astype(BF16), wk_ref[...])
    o_ref[:, kvl:] = qr


def _dec_q(q, wk, tabs, heads):
    m = q.shape[0]
    kvl = wk.shape[0]
    tab_spec = pl.BlockSpec((m, LANES), lambda h: (0, 0))
    return pl.pallas_call(
        functools.partial(_dec_q_kernel, kvl=kvl),
        grid=(heads,),
        in_specs=[pl.BlockSpec((m, MLA_QPAD), lambda h: (0, h)),
                  pl.BlockSpec((kvl, MLA_NOPE), lambda h: (0, h)),
                  tab_spec, tab_spec, tab_spec],
        out_specs=pl.BlockSpec((m, kvl + LANES), lambda h: (0, h)),
        out_shape=jax.ShapeDtypeStruct((m, heads * (kvl + LANES)), F32),
        compiler_params=_cparams("parallel"),
        name="dec_q",
    )(q, wk, *tabs)


def _dec_attn_kernel(pt_ref, q_ref, lat_new_ref, kr_new_ref, lat_hbm, kr_hbm, o_ref, lat_buf, kr_buf, sem_lat, sem_kr,
                     *, layer, pg, groups, kvl, scale):
    b = pl.program_id(0)
    total = pl.num_programs(0) * groups

    def page_copies(slot, j, page):
        return (pltpu.make_async_copy(lat_hbm.at[layer, page], lat_buf.at[slot, j], sem_lat.at[slot]),
                pltpu.make_async_copy(kr_hbm.at[layer, page], kr_buf.at[slot, j], sem_kr.at[slot]))

    def start_group(t, slot):
        row = t // groups
        col0 = (t % groups) * pg
        for j in range(pg):
            for c in page_copies(slot, j, pt_ref[row, col0 + j]):
                c.start()

    def wait_group(slot):
        for j in range(pg):
            for c in page_copies(slot, j, 0):
                c.wait()

    @pl.when(b == 0)
    def _():
        start_group(0, 0)

    q = q_ref[0]
    q_lat = q[:, :kvl]
    q_rope = q[:, kvl:kvl + MLA_ROPE]
    r = b % 8
    lat_new = lat_new_ref[pl.ds(r, 1), :]
    kr_new = kr_new_ref[pl.ds(r, 1), :]
    m0 = (jnp.sum(q_lat * lat_new, axis=1, keepdims=True)
          + jnp.sum(q_rope * kr_new, axis=1, keepdims=True)) * scale
    l0 = jnp.ones_like(m0)
    acc0 = jnp.broadcast_to(lat_new, (q.shape[0], kvl))
    qlb = q_lat.astype(BF16)
    qrb = q_rope.astype(BF16)

    def body(g, carry):
        m_prev, l_prev, acc = carry
        t = b * groups + g
        slot = t % 2
        start_group(jnp.minimum(t + 1, total - 1), 1 - slot)
        wait_group(slot)
        lats, ss = [], []
        for j in range(pg):
            lat = lat_buf[slot, j].astype(BF16)
            kr_t = kr_buf[slot, j].astype(BF16)
            lats.append(lat)
            ss.append((_dot_nt(qlb, lat) + jnp.dot(qrb, kr_t, preferred_element_type=F32)) * scale)
        m_new = m_prev
        for s in ss:
            m_new = jnp.maximum(m_new, jnp.max(s, axis=1, keepdims=True))
        corr = jnp.exp(m_prev - m_new)
        l_new = l_prev * corr
        acc = acc * corr
        for j in range(pg):
            pe = jnp.exp(ss[j] - m_new)
            l_new = l_new + jnp.sum(pe, axis=1, keepdims=True)
            acc = acc + jnp.dot(pe.astype(BF16), lats[j], preferred_element_type=F32)
        return m_new, l_new, acc

    _, l_fin, acc_fin = lax.fori_loop(0, groups, body, (m0, l0, acc0))
    o_ref[0] = acc_fin / l_fin

    @pl.when(b == pl.num_programs(0) - 1)
    def _():
        wait_group(total % 2)


def _dec_attn(page_table, q3, lat_new, kr_new, cache_lat, cache_kr_t, layer):
    m, heads, qw = q3.shape
    kvl = cache_lat.shape[-1]
    n_pages = page_table.shape[1]
    pg = min(DEC_PAGES_PER_STEP, n_pages)
    groups = n_pages // pg
    assert groups * pg == n_pages
    scale = (MLA_NOPE + MLA_ROPE) ** -0.5
    return pl.pallas_call(
        functools.partial(_dec_attn_kernel, layer=layer, pg=pg, groups=groups, kvl=kvl, scale=scale),
        grid_spec=pltpu.PrefetchScalarGridSpec(
            num_scalar_prefetch=1,
            grid=(m,),
            in_specs=[pl.BlockSpec((1, heads, qw), lambda b, pt: (b, 0, 0)),
                      pl.BlockSpec((8, kvl), lambda b, pt: (b // 8, 0)),
                      pl.BlockSpec((8, MLA_ROPE), lambda b, pt: (b // 8, 0)),
                      pl.BlockSpec(memory_space=pl.ANY),
                      pl.BlockSpec(memory_space=pl.ANY)],
            out_specs=pl.BlockSpec((1, heads, kvl), lambda b, pt: (b, 0, 0)),
            scratch_shapes=[pltpu.VMEM((2, pg, PAGE_SIZE, kvl), F32),
                            pltpu.VMEM((2, pg, MLA_ROPE, PAGE_SIZE), F32),
                            pltpu.SemaphoreType.DMA((2,)),
                            pltpu.SemaphoreType.DMA((2,))]),
        out_shape=jax.ShapeDtypeStruct((m, heads, kvl), F32),
        compiler_params=_cparams("arbitrary"),
        name="dec_attn",
    )(page_table, q3, lat_new, kr_new, cache_lat, cache_kr_t)


def _dec_ov_kernel(o_ref, wv_ref, gate_ref, out_ref):
    out_ref[...] = (jnp.dot(o_ref[...].astype(BF16), wv_ref[...], preferred_element_type=F32)
                    * _silu(gate_ref[...]))


def _dec_ov(o_lat2, wv, p, gate_blk0, heads):
    m = o_lat2.shape[0]
    kvl = wv.shape[0]
    return pl.pallas_call(
        _dec_ov_kernel,
        grid=(heads,),
        in_specs=[pl.BlockSpec((m, kvl), lambda h: (0, h)),
                  pl.BlockSpec((kvl, MLA_V), lambda h: (0, h)),
                  pl.BlockSpec((m, MLA_V), lambda h: (0, gate_blk0 + h))],
        out_specs=pl.BlockSpec((m, MLA_V), lambda h: (0, h)),
        out_shape=jax.ShapeDtypeStruct((m, heads * MLA_V), F32),
        compiler_params=_cparams("parallel"),
        name="dec_ov",
    )(o_lat2, wv, p)


def _post_kernel(y_ref, h_ref, g_ref, o_ref):
    y = y_ref[...]
    ms = jnp.mean(y * y, axis=-1, keepdims=True)
    o_ref[...] = h_ref[...] + y * lax.rsqrt(ms + EPS) * g_ref[...]


def _post(y, h, g, bm):
    m, d = y.shape
    return pl.pallas_call(
        _post_kernel,
        grid=(m // bm,),
        in_specs=[pl.BlockSpec((bm, d), lambda i: (i, 0)),
                  pl.BlockSpec((bm, d), lambda i: (i, 0)),
                  pl.BlockSpec((1, d), lambda i: (0, 0))],
        out_specs=pl.BlockSpec((bm, d), lambda i: (i, 0)),
        out_shape=jax.ShapeDtypeStruct((m, d), F32),
        compiler_params=_cparams("parallel"),
        name="post_norm",
    )(y, h, g)


def _layout(segments):
    off, offs = 0, {}
    for name, width, block in segments:
        off = -(-off // block) * block
        offs[name] = off
        off += width
    return offs, -(-off // LANES) * LANES


def _rope_tables(pos):
    half = MLA_ROPE // 2
    inv_freq = ROPE_THETA ** (-jnp.arange(half, dtype=F32) / half)
    ang = pos.astype(F32)[:, None] * inv_freq[None, :]
    cos, sin = jnp.cos(ang), jnp.sin(ang)
    zeros = jnp.zeros_like(cos)
    pad = jnp.zeros((pos.shape[0], LANES - MLA_ROPE), F32)
    tc = jnp.concatenate([cos, cos, pad], axis=1)
    ta = jnp.concatenate([-sin, zeros, pad], axis=1)
    tb = jnp.concatenate([zeros, sin, pad], axis=1)
    return tc, ta, tb


def _pick(n, prefs):
    for b in prefs:
        if n % b == 0:
            return b
    return n


def kernel(x_prompt, x_sample, cache_mla_latent, cache_mla_krope, state_gla, state_ssm, state_conv, page_table,
           norm_pre, norm_post, w_in, mla_q_norm, mla_w_qb, mla_kv_norm, mla_w_kb, mla_w_vb, gla_w_a2, gla_b_a,
           gla_norm, ssm_conv_w, ssm_conv_b, ssm_dt_bias, ssm_a_log, ssm_d, ssm_norm, w_out):
    depth = w_in.shape[0]
    bp, lp, d = x_prompt.shape
    bs, ls, _ = x_sample.shape
    assert ls == 1
    q_lora = mla_q_norm.shape[1]
    kvl = mla_kv_norm.shape[1]
    heads = mla_w_kb.shape[2]
    dv = gla_norm.shape[1]
    dk = gla_w_a2.shape[2] // GLA_HEADS
    s_heads = ssm_d.shape[1]
    s_width = s_heads * SSM_HEAD_DIM
    conv_ch = ssm_conv_w.shape[2]
    mla_w = heads * MLA_V
    gla_w = GLA_HEADS * dv
    assert dk == LANES and dv % LANES == 0 and kvl % LANES == 0

    in_sizes = (q_lora, kvl + MLA_ROPE, GLA_HEADS * dk, GLA_HEADS * dk, gla_w, gla_w_a2.shape[1], conv_ch, s_heads,
                mla_w, gla_w, s_width)
    names = ("q_a", "kv_a", "g_q", "g_k", "g_v", "g_a", "xbc", "dt", "gate_mla", "gate_gla", "z")
    src, acc = {}, 0
    for nm, sz in zip(names, in_sizes):
        src[nm] = (acc, sz)
        acc += sz
    segs = [("q_a", q_lora, q_lora), ("z", s_width, s_width), ("g_v", gla_w, gla_w), ("gate_gla", gla_w, gla_w),
            ("gate_mla", mla_w, MLA_V), ("xbc", conv_ch, conv_ch), ("ckv", kvl, kvl),
            ("g_q", GLA_HEADS * dk, GLA_HEADS * dk), ("g_k", GLA_HEADS * dk, GLA_HEADS * dk)]
    offs, n_main = _layout(segs)
    blk = {"q_a": offs["q_a"] // q_lora, "z": offs["z"] // s_width, "g_v": offs["g_v"] // dv,
           "gate_gla": offs["gate_gla"] // dv, "gate_mla": offs["gate_mla"] // MLA_V,
           "xbc": offs["xbc"] // conv_ch, "ckv": offs["ckv"] // kvl, "g_q": offs["g_q"] // dk,
           "g_k": offs["g_k"] // dk}

    def w_in_main(wl):
        cols = jnp.zeros((d, n_main), BF16)
        for nm, width, _ in segs:
            s0 = src["kv_a"][0] if nm == "ckv" else src[nm][0]
            cols = lax.dynamic_update_slice(cols, wl[:, s0:s0 + width].astype(BF16), (0, offs[nm]))
        return cols

    def w_in_small(wl):
        kr0 = src["kv_a"][0] + kvl
        parts = [wl[:, kr0:kr0 + MLA_ROPE], wl[:, src["g_a"][0]:src["g_a"][0] + src["g_a"][1]],
                 wl[:, src["dt"][0]:src["dt"][0] + s_heads]]
        assert src["g_a"][1] == SMALL_DT - SMALL_GA and SMALL_DT + s_heads <= SMALL_W
        used = MLA_ROPE + src["g_a"][1] + s_heads
        parts.append(jnp.zeros((d, SMALL_W - used), F32))
        return jnp.concatenate(parts, axis=1).astype(BF16)

    mp, ms_ = bp * lp, bs
    n_pages = page_table.shape[1]
    tabs_p = _rope_tables(jnp.arange(lp))
    tabs_s = _rope_tables(jnp.full((ms_,), n_pages * PAGE_SIZE))
    hp = x_prompt.reshape(mp, d)
    hs = x_sample.reshape(ms_, d)
    bn_main = _pick(n_main, (1024, 512, 256, 128))
    bn_out = _pick(d, (1024, 512, 256, 128))
    bm_p = _pick(mp, (1024, 512, 256))
    bm_row = _pick(mp, (256,))
    bm_kv = _pick(lp, (256, 128))
    conv0_p = jnp.zeros((bp, SSM_CONV - 1, conv_ch), F32)
    ssm0_p = jnp.zeros((bp, s_heads, SSM_HEAD_DIM, SSM_D_STATE), F32)
    cache_kr_t = jnp.swapaxes(cache_mla_krope, 2, 3)
    conv_t = jnp.swapaxes(state_conv, 1, 2)
    gla_new_s = jnp.zeros(state_gla.shape, F32)
    ssm_new_s = jnp.zeros(state_ssm.shape, F32)
    conv_new_s = jnp.zeros(conv_t.shape, F32)

    rows_p, rows_s = [], []
    for l in range(depth):
        wl = w_in[l]
        w_main = w_in_main(wl)
        w_small = w_in_small(wl)
        w_qb = jnp.pad(mla_w_qb[l].reshape(q_lora, heads, MLA_NOPE + MLA_ROPE),
                       ((0, 0), (0, 0), (0, MLA_QPAD - MLA_NOPE - MLA_ROPE))).reshape(q_lora, heads * MLA_QPAD)
        w_qb = w_qb.astype(BF16)
        w_kb = mla_w_kb[l].reshape(kvl, heads * MLA_NOPE).astype(BF16)
        w_vb = mla_w_vb[l].reshape(kvl, heads * MLA_V).astype(BF16)
        w_o = w_out[l].astype(BF16)
        wa_pad = jnp.zeros((SMALL_W, GLA_HEADS * dk), F32)
        wa_pad = lax.dynamic_update_slice(wa_pad, gla_w_a2[l], (SMALL_GA, 0)).astype(BF16)
        ba = gla_b_a[l][None, :]
        gn_gla = gla_norm[l][None, :]
        sw = {"cw": ssm_conv_w[l], "cb": ssm_conv_b[l][None, :], "dtb": ssm_dt_bias[l][None, :],
              "dtbc": ssm_dt_bias[l][:, None], "al": ssm_a_log[l][None, :], "alc": ssm_a_log[l][:, None],
              "dx": jnp.repeat(ssm_d[l], SSM_HEAD_DIM)[None, :], "gn": ssm_norm[l][None, :],
              "dtbx": jnp.repeat(ssm_dt_bias[l], SSM_HEAD_DIM)[None, :],
              "alx": jnp.repeat(ssm_a_log[l], SSM_HEAD_DIM)[None, :]}
        g_pre = norm_pre[l][None, :]
        g_post = norm_post[l][None, :]
        g_q = mla_q_norm[l][None, :]
        g_kv = mla_kv_norm[l][None, :]

        u, small = _norm_small(hp, g_pre, w_small, bm_row)
        p = _matmul([u], w_main, [d], bm_p, bn_main)
        q = _norm_matmul(p, blk["q_a"], g_q, w_qb, _pick(mp, (512, 256)), _pick(heads * MLA_QPAD, (1024, 512, 256)))
        lat, kr, kcat, v = _mla_kv(p, blk["ckv"], small, g_kv, tabs_p, w_kb, w_vb, bm_kv, True)
        o_mla = _flash(q, kcat, v, p, blk["gate_mla"], tabs_p, bp, lp, heads)
        o_gla, gla_new = _gla_prompt(p, small, blk, wa_pad, ba, gn_gla, bp, lp, dk, dv)
        dtt = small[:, SMALL_DT:SMALL_DT + s_heads].reshape(bp, lp, s_heads).transpose(0, 2, 1)
        o_ssm, ssm_new, conv_new = _ssd_prompt(p, small, dtt, blk, sw, conv0_p, ssm0_p, bp, lp)
        y = _matmul([o_mla, o_gla, o_ssm], w_o, [mla_w, gla_w, s_width], bm_p, bn_out)
        hp = _post(y, hp, g_post, bm_row)
        rows_p.append((lat.reshape(bp, lp, kvl), kr.reshape(bp, lp, MLA_ROPE), gla_new, ssm_new, conv_new))

        u, small = _norm_small(hs, g_pre, w_small, ms_)
        p = _matmul([u], w_main, [d], ms_, bn_main)
        q = _norm_matmul(p, blk["q_a"], g_q, w_qb, ms_, _pick(heads * MLA_QPAD, (1024, 512, 256)))
        lat, kr = _mla_kv(p, blk["ckv"], small, g_kv, tabs_s, w_kb, w_vb, ms_, False)
        qcat = _dec_q(q, w_kb, tabs_s, heads).reshape(ms_, heads, kvl + LANES)
        o_lat = _dec_attn(page_table, qcat, lat, kr, cache_mla_latent, cache_kr_t, l)
        o_mla = _dec_ov(o_lat.reshape(ms_, heads * kvl), w_vb, p, blk["gate_mla"], heads)
        o_gla, gla_new_s = _gla_dec(p, small, blk, wa_pad, ba, gn_gla, state_gla, gla_new_s, l, dk, dv)
        dtx = jnp.repeat(small[:, SMALL_DT:SMALL_DT + s_heads], SSM_HEAD_DIM, axis=1)
        o_ssm, ssm_new_s, conv_new_s = _ssd_dec(p, dtx, blk, sw, conv_t, state_ssm, (ssm_new_s, conv_new_s), l)
        y = _matmul([o_mla, o_gla, o_ssm], w_o, [mla_w, gla_w, s_width], ms_, bn_out)
        hs = _post(y, hs, g_post, ms_)
        rows_s.append((lat.reshape(bs, 1, kvl), kr.reshape(bs, 1, MLA_ROPE)))

    outs = [hp.reshape(bp, lp, d), hs.reshape(bs, 1, d)]
    for i in range(5):
        outs.append(jnp.stack([r[i] for r in rows_p]))
    for i in range(2):
        outs.append(jnp.stack([r[i] for r in rows_s]))
    outs += [gla_new_s, ssm_new_s, jnp.swapaxes(conv_new_s, 1, 2)]
    return tuple(outs)
```

```python
import functools
import math

import jax
import jax.numpy as jnp
from jax import lax
from jax.experimental import pallas as pl
from jax.experimental.pallas import tpu as pltpu

F32 = jnp.float32
BF16 = jnp.bfloat16

EPS = 1e-6
MLA_V = 128
MLA_NOPE = 128
MLA_ROPE = 64
MLA_QPAD = 256
ROPE_THETA = 10000.0
GLA_HEADS = 4
GLA_GATE_NORM = 16.0
GLA_CHUNK = 32
SSM_HEAD_DIM = 64
SSM_GROUPS = 2
SSM_D_STATE = 128
SSM_CONV = 4
PAGE_SIZE = 128
LANES = 128
SMALL_W = 128
SMALL_KROPE = 0
SMALL_GA = 64
SMALL_DT = 80

GLA_TILE = 256
GLA_HEADS_PER_STEP = 4
SSD_TILE = 256
FLASH_BQ = 512
DEC_PAGES_PER_STEP = 16
DEC_SLOTS = 3
DEC_AHEAD = DEC_SLOTS - 1
DEC_ROWS = 8
VMEM_LIMIT = 52 * 1024 * 1024


def _cparams(*sem):
    return pltpu.CompilerParams(dimension_semantics=sem, vmem_limit_bytes=VMEM_LIMIT)


def _silu(x):
    return x * (1.0 / (1.0 + jnp.exp(-x)))


def _softplus(x):
    return jnp.maximum(x, 0.0) + jnp.log1p(jnp.exp(-jnp.abs(x)))


def _split3(x):
    a = x.astype(BF16)
    r = x - a.astype(F32)
    b = r.astype(BF16)
    c = (r - b.astype(F32)).astype(BF16)
    return a, b, c


def _tri_rows(t, block):
    i = lax.broadcasted_iota(jnp.int32, (t, t), 0)
    j = lax.broadcasted_iota(jnp.int32, (t, t), 1)
    return ((j <= i) & ((i // block) == (j // block))).astype(BF16)


def _tri_cols(t):
    i = lax.broadcasted_iota(jnp.int32, (t, t), 0)
    j = lax.broadcasted_iota(jnp.int32, (t, t), 1)
    return (i <= j).astype(BF16)


def _cumsum_rows(x, tri):
    a, b, c = _split3(x)
    d = functools.partial(jnp.dot, preferred_element_type=F32)
    return d(tri, a) + d(tri, b) + d(tri, c)


def _cumsum_lanes(x, tri_t):
    a, b, c = _split3(x)
    d = functools.partial(jnp.dot, preferred_element_type=F32)
    return d(a, tri_t) + d(b, tri_t) + d(c, tri_t)


def _dot_nt(a, b):
    return lax.dot_general(a, b, (((1,), (1,)), ((), ())), preferred_element_type=F32)


def _dot_tn(a, b):
    return lax.dot_general(a, b, (((0,), (0,)), ((), ())), preferred_element_type=F32)


def _col_bcast(row):
    return jnp.broadcast_to(row, (LANES, LANES)).T


def _rope128(x, tc, ta, tb):
    return x * tc + pltpu.roll(x, 96, axis=1) * ta + pltpu.roll(x, 32, axis=1) * tb


def _norm_small_kernel(x_ref, g_ref, ws_ref, u_ref, s_ref):
    x = x_ref[...]
    ms = jnp.mean(x * x, axis=-1, keepdims=True)
    u = (x * lax.rsqrt(ms + EPS) * g_ref[...]).astype(BF16)
    u_ref[...] = u
    s_ref[...] = jnp.dot(u, ws_ref[...], preferred_element_type=F32)


def _norm_small(x, g, w_small, bm):
    m, d = x.shape
    return pl.pallas_call(
        _norm_small_kernel,
        grid=(m // bm,),
        in_specs=[pl.BlockSpec((bm, d), lambda i: (i, 0)),
                  pl.BlockSpec((1, d), lambda i: (0, 0)),
                  pl.BlockSpec((d, SMALL_W), lambda i: (0, 0))],
        out_specs=[pl.BlockSpec((bm, d), lambda i: (i, 0)),
                   pl.BlockSpec((bm, SMALL_W), lambda i: (i, 0))],
        out_shape=[jax.ShapeDtypeStruct((m, d), BF16), jax.ShapeDtypeStruct((m, SMALL_W), F32)],
        compiler_params=_cparams("parallel"),
        name="norm_small",
    )(x, g, w_small)


def _mm_kernel(*refs, n):
    o_ref = refs[2 * n]
    acc = None
    for i in range(n):
        p = jnp.dot(refs[i][...].astype(BF16), refs[n + i][...], preferred_element_type=F32)
        acc = p if acc is None else acc + p
    o_ref[...] = acc.astype(o_ref.dtype)


def _matmul(lhs_list, w, k_sizes, bm, bn, out_dtype=F32):
    n = len(lhs_list)
    m = lhs_list[0].shape[0]
    ncols = w.shape[1]
    in_specs, k0 = [], 0
    for a in lhs_list:
        in_specs.append(pl.BlockSpec((bm, a.shape[1]), lambda i, j: (i, 0)))
    for ks in k_sizes:
        assert k0 % ks == 0
        in_specs.append(pl.BlockSpec((ks, bn), functools.partial(lambda i, j, kb: (kb, j), kb=k0 // ks)))
        k0 += ks
    return pl.pallas_call(
        functools.partial(_mm_kernel, n=n),
        grid=(m // bm, ncols // bn),
        in_specs=in_specs,
        out_specs=pl.BlockSpec((bm, bn), lambda i, j: (i, j)),
        out_shape=jax.ShapeDtypeStruct((m, ncols), out_dtype),
        compiler_params=_cparams("parallel", "parallel"),
        name="matmul%d" % n,
    )(*lhs_list, *([w] * n))


def _norm_mm_kernel(x_ref, g_ref, w_ref, o_ref):
    x = x_ref[...]
    ms = jnp.mean(x * x, axis=-1, keepdims=True)
    xn = (x * lax.rsqrt(ms + EPS) * g_ref[...]).astype(BF16)
    o_ref[...] = jnp.dot(xn, w_ref[...], preferred_element_type=F32)


def _norm_matmul(p, col_blk, g, w, bm, bn):
    m = p.shape[0]
    k, ncols = w.shape
    return pl.pallas_call(
        _norm_mm_kernel,
        grid=(m // bm, ncols // bn),
        in_specs=[pl.BlockSpec((bm, k), lambda i, j: (i, col_blk)),
                  pl.BlockSpec((1, k), lambda i, j: (0, 0)),
                  pl.BlockSpec((k, bn), lambda i, j: (0, j))],
        out_specs=pl.BlockSpec((bm, bn), lambda i, j: (i, j)),
        out_shape=jax.ShapeDtypeStruct((m, ncols), F32),
        compiler_params=_cparams("parallel", "parallel"),
        name="q_proj",
    )(p, g, w)


def _mla_kv_kernel(ckv_ref, small_ref, g_ref, tc_ref, ta_ref, tb_ref, *rest, expand, heads):
    if expand:
        wk_ref, wv_ref, lat_ref, kr_ref, kcat_ref, v_ref = rest
    else:
        lat_ref, kr_ref = rest
    x = ckv_ref[...]
    ms = jnp.mean(x * x, axis=-1, keepdims=True)
    c = x * lax.rsqrt(ms + EPS) * g_ref[...]
    lat_ref[...] = c
    kr = _rope128(small_ref[...], tc_ref[...], ta_ref[...], tb_ref[...])
    kr_ref[...] = kr[:, :MLA_ROPE]
    if expand:
        cb = c.astype(BF16)
        kn = jnp.dot(cb, wk_ref[...], preferred_element_type=F32).astype(BF16)
        v_ref[...] = jnp.dot(cb, wv_ref[...], preferred_element_type=F32).astype(BF16)
        krb = kr.astype(BF16)
        for h in range(heads):
            kcat_ref[:, h * MLA_QPAD:h * MLA_QPAD + MLA_NOPE] = kn[:, h * MLA_NOPE:(h + 1) * MLA_NOPE]
            kcat_ref[:, h * MLA_QPAD + MLA_NOPE:(h + 1) * MLA_QPAD] = krb


def _mla_kv(p, ckv_blk, small, g, tabs, wk, wv, bm, expand):
    m = p.shape[0]
    kvl = g.shape[1]
    nt = tabs[0].shape[0] // bm
    heads = wk.shape[1] // MLA_NOPE
    tab_spec = pl.BlockSpec((bm, LANES), lambda i: (i % nt, 0))
    in_specs = [pl.BlockSpec((bm, kvl), lambda i: (i, ckv_blk)),
                pl.BlockSpec((bm, SMALL_W), lambda i: (i, 0)),
                pl.BlockSpec((1, kvl), lambda i: (0, 0)),
                tab_spec, tab_spec, tab_spec]
    out_specs = [pl.BlockSpec((bm, kvl), lambda i: (i, 0)),
                 pl.BlockSpec((bm, MLA_ROPE), lambda i: (i, 0))]
    out_shape = [jax.ShapeDtypeStruct((m, kvl), F32), jax.ShapeDtypeStruct((m, MLA_ROPE), F32)]
    args = [p, small, g, *tabs]
    if expand:
        in_specs += [pl.BlockSpec(wk.shape, lambda i: (0, 0)), pl.BlockSpec(wv.shape, lambda i: (0, 0))]
        out_specs += [pl.BlockSpec((bm, heads * MLA_QPAD), lambda i: (i, 0)),
                      pl.BlockSpec((bm, heads * MLA_V), lambda i: (i, 0))]
        out_shape += [jax.ShapeDtypeStruct((m, heads * MLA_QPAD), BF16),
                      jax.ShapeDtypeStruct((m, heads * MLA_V), BF16)]
        args += [wk, wv]
    return pl.pallas_call(
        functools.partial(_mla_kv_kernel, expand=expand, heads=heads),
        grid=(m // bm,),
        in_specs=in_specs, out_specs=out_specs, out_shape=out_shape,
        compiler_params=_cparams("parallel"),
        name="mla_kv",
    )(*args)


def _flash_kernel(q_ref, k_ref, v_ref, gate_ref, tc_ref, ta_ref, tb_ref, o_ref, qs_ref, *, seq, bq, scale):
    q = q_ref[...]
    qr = _rope128(q[:, MLA_NOPE:], tc_ref[...], ta_ref[...], tb_ref[...])
    qs_ref[:, :MLA_NOPE] = (q[:, :MLA_NOPE] * scale).astype(BF16)
    qs_ref[:, MLA_NOPE:] = (qr * scale).astype(BF16)
    ii = lax.broadcasted_iota(jnp.int32, (bq, bq), 0)
    jj = lax.broadcasted_iota(jnp.int32, (bq, bq), 1)
    for i in range(seq // bq):
        rows = slice(i * bq, (i + 1) * bq)
        qi = qs_ref[rows, :]
        s_d = jnp.where(jj <= ii, _dot_nt(qi, k_ref[rows, :]), -jnp.inf)
        m = jnp.max(s_d, axis=1, keepdims=True)
        if i > 0:
            s_a = _dot_nt(qi, k_ref[0:i * bq, :])
            m = jnp.maximum(m, jnp.max(s_a, axis=1, keepdims=True))
        p_d = jnp.exp(s_d - m)
        l = jnp.sum(p_d, axis=1, keepdims=True)
        o = jnp.dot(p_d.astype(BF16), v_ref[rows, :], preferred_element_type=F32)
        if i > 0:
            p_a = jnp.exp(s_a - m)
            l = l + jnp.sum(p_a, axis=1, keepdims=True)
            o = o + jnp.dot(p_a.astype(BF16), v_ref[0:i * bq, :], preferred_element_type=F32)
        o_ref[rows, :] = (o * (1.0 / l) * _silu(gate_ref[rows, :])).astype(o_ref.dtype)


def _flash(q, kcat, v, p, gate_blk0, tabs, batch, seq, heads):
    m = q.shape[0]
    bq = min(FLASH_BQ, seq)
    scale = (MLA_NOPE + MLA_ROPE) ** -0.5
    tab_spec = pl.BlockSpec((seq, LANES), lambda b, h: (0, 0))
    return pl.pallas_call(
        functools.partial(_flash_kernel, seq=seq, bq=bq, scale=scale),
        grid=(batch, heads),
        in_specs=[pl.BlockSpec((seq, MLA_QPAD), lambda b, h: (b, h)),
                  pl.BlockSpec((seq, MLA_QPAD), lambda b, h: (b, h)),
                  pl.BlockSpec((seq, MLA_V), lambda b, h: (b, h)),
                  pl.BlockSpec((seq, MLA_V), lambda b, h: (b, gate_blk0 + h)),
                  tab_spec, tab_spec, tab_spec],
        out_specs=pl.BlockSpec((seq, MLA_V), lambda b, h: (b, h)),
        out_shape=jax.ShapeDtypeStruct((m, heads * MLA_V), BF16),
        scratch_shapes=[pltpu.VMEM((seq, MLA_QPAD), BF16)],
        compiler_params=_cparams("parallel", "parallel"),
        name="flash",
    )(q, kcat, v, p, *tabs)


def _log_decay(small, wa_ref, ba_ref):
    x = jnp.dot(small.astype(BF16), wa_ref[...], preferred_element_type=F32) + ba_ref[...]
    return -_softplus(-x) * (1.0 / GLA_GATE_NORM)


def _gla_kernel(q_ref, k_ref, v_ref, small_ref, gate_ref, wa_ref, ba_ref, gn_ref, o_ref, st_ref, s_ref,
                *, tile, dk, dv, hpb):
    c = pl.program_id(2)

    @pl.when(c == 0)
    def _():
        s_ref[...] = jnp.zeros(s_ref.shape, F32)

    la = _log_decay(small_ref[...], wa_ref, ba_ref)
    bcs = _cumsum_rows(la, _tri_rows(tile, GLA_CHUNK))
    ii = lax.broadcasted_iota(jnp.int32, (GLA_CHUNK, GLA_CHUNK), 0)
    jj = lax.broadcasted_iota(jnp.int32, (GLA_CHUNK, GLA_CHUNK), 1)
    scale = dk ** -0.5
    outs = [[] for _ in range(hpb)]
    for i in range(tile // GLA_CHUNK):
        sl = slice(i * GLA_CHUNK, (i + 1) * GLA_CHUNK)
        for h in range(hpb):
            ks = slice(h * dk, (h + 1) * dk)
            bi = bcs[sl, ks]
            ki = k_ref[sl, ks]
            vi = v_ref[sl, h * dv:(h + 1) * dv].astype(BF16)
            qd = (q_ref[sl, ks] * scale * jnp.exp(bi)).astype(BF16)
            kd = (ki * jnp.exp(-bi)).astype(BF16)
            att = jnp.where(jj <= ii, _dot_nt(qd, kd), 0.0).astype(BF16)
            s_old = s_ref[h]
            o = (jnp.dot(qd, s_old.astype(BF16), preferred_element_type=F32)
                 + jnp.dot(att, vi, preferred_element_type=F32))
            outs[h].append(o)
            bl = bi[GLA_CHUNK - 1:GLA_CHUNK, :]
            kv = _dot_tn((ki * jnp.exp(bl - bi)).astype(BF16), vi)
            dcol = _col_bcast(jnp.exp(bl))
            for j in range(dv // LANES):
                ls = slice(j * LANES, (j + 1) * LANES)
                s_ref[h, :, ls] = dcol * s_old[:, ls] + kv[:, ls]
    for h in range(hpb):
        vs = slice(h * dv, (h + 1) * dv)
        o = jnp.concatenate(outs[h], axis=0)
        ms = jnp.mean(o * o, axis=-1, keepdims=True)
        o = o * lax.rsqrt(ms + EPS) * gn_ref[...]
        o_ref[:, vs] = (o * _silu(gate_ref[:, vs])).astype(o_ref.dtype)

    @pl.when(c == pl.num_programs(2) - 1)
    def _():
        st_ref[0] = s_ref[...]


def _gla_prompt(p, small, blk, wa_pad, ba, gn, batch, seq, dk, dv):
    m = p.shape[0]
    tile = min(GLA_TILE, seq)
    nb = seq // tile
    hpb = min(GLA_HEADS_PER_STEP, GLA_HEADS)
    hg = GLA_HEADS // hpb
    wk, wv = hpb * dk, hpb * dv
    row = lambda b, h, c: b * nb + c
    return pl.pallas_call(
        functools.partial(_gla_kernel, tile=tile, dk=dk, dv=dv, hpb=hpb),
        grid=(batch, hg, nb),
        in_specs=[pl.BlockSpec((tile, wk), lambda b, h, c: (row(b, h, c), blk["g_q"] // hpb + h)),
                  pl.BlockSpec((tile, wk), lambda b, h, c: (row(b, h, c), blk["g_k"] // hpb + h)),
                  pl.BlockSpec((tile, wv), lambda b, h, c: (row(b, h, c), blk["g_v"] // hpb + h)),
                  pl.BlockSpec((tile, SMALL_W), lambda b, h, c: (row(b, h, c), 0)),
                  pl.BlockSpec((tile, wv), lambda b, h, c: (row(b, h, c), blk["gate_gla"] // hpb + h)),
                  pl.BlockSpec((SMALL_W, wk), lambda b, h, c: (0, h)),
                  pl.BlockSpec((1, wk), lambda b, h, c: (0, h)),
                  pl.BlockSpec((1, dv), lambda b, h, c: (0, 0))],
        out_specs=[pl.BlockSpec((tile, wv), lambda b, h, c: (row(b, h, c), h)),
                   pl.BlockSpec((1, hpb, dk, dv), lambda b, h, c: (b, h, 0, 0))],
        out_shape=[jax.ShapeDtypeStruct((m, GLA_HEADS * dv), BF16),
                   jax.ShapeDtypeStruct((batch, GLA_HEADS, dk, dv), F32)],
        scratch_shapes=[pltpu.VMEM((hpb, dk, dv), F32)],
        compiler_params=_cparams("parallel", "parallel", "arbitrary"),
        name="gla_prompt",
    )(p, p, p, small, p, wa_pad, ba, gn)


def _gla_dec_kernel(q_ref, k_ref, v_ref, small_ref, gate_ref, wa_ref, ba_ref, gn_ref, s0_ref, *rest, dk, dv):
    o_ref, st_ref, ob_ref = rest[-3:]
    la = _log_decay(small_ref[...], wa_ref, ba_ref)
    a = jnp.exp(la)
    q = q_ref[...] * (dk ** -0.5)
    k = k_ref[...]
    v = v_ref[...]
    for r in range(DEC_ROWS):
        for h in range(GLA_HEADS):
            ks = slice(h * dk, (h + 1) * dk)
            acol = _col_bcast(a[r:r + 1, ks])
            kcol = _col_bcast(k[r:r + 1, ks])
            qcol = _col_bcast(q[r:r + 1, ks])
            for j in range(dv // LANES):
                ls = slice(j * LANES, (j + 1) * LANES)
                vrow = v[r:r + 1, h * dv + j * LANES:h * dv + (j + 1) * LANES]
                s_new = acol * s0_ref[r, h, :, ls] + kcol * vrow
                st_ref[r, h, :, ls] = s_new
                ob_ref[r:r + 1, h * dv + j * LANES:h * dv + (j + 1) * LANES] = jnp.sum(
                    qcol * s_new, axis=0, keepdims=True)
    for h in range(GLA_HEADS):
        vs = slice(h * dv, (h + 1) * dv)
        o = ob_ref[:, vs]
        ms = jnp.mean(o * o, axis=-1, keepdims=True)
        o_ref[:, vs] = o * lax.rsqrt(ms + EPS) * gn_ref[...] * _silu(gate_ref[:, vs])


def _gla_dec(p, small, blk, wa_pad, ba, gn, states, prev_new, layer, dk, dv):
    m = p.shape[0]
    hk, hv = GLA_HEADS * dk, GLA_HEADS * dv
    st_spec = pl.BlockSpec((None, DEC_ROWS, GLA_HEADS, dk, dv), lambda i: (layer, i, 0, 0, 0))
    in_specs = [pl.BlockSpec((DEC_ROWS, hk), lambda i: (i, blk["g_q"] * dk // hk)),
                pl.BlockSpec((DEC_ROWS, hk), lambda i: (i, blk["g_k"] * dk // hk)),
                pl.BlockSpec((DEC_ROWS, hv), lambda i: (i, blk["g_v"] * dv // hv)),
                pl.BlockSpec((DEC_ROWS, SMALL_W), lambda i: (i, 0)),
                pl.BlockSpec((DEC_ROWS, hv), lambda i: (i, blk["gate_gla"] * dv // hv)),
                pl.BlockSpec((SMALL_W, hk), lambda i: (0, 0)),
                pl.BlockSpec((1, hk), lambda i: (0, 0)),
                pl.BlockSpec((1, dv), lambda i: (0, 0)),
                st_spec]
    args = [p, p, p, small, p, wa_pad, ba, gn, states]
    aliases = {}
    if prev_new is not None:
        in_specs.append(pl.BlockSpec(memory_space=pl.ANY))
        aliases = {len(args): 1}
        args.append(prev_new)
    return pl.pallas_call(
        functools.partial(_gla_dec_kernel, dk=dk, dv=dv),
        grid=(m // DEC_ROWS,),
        in_specs=in_specs,
        out_specs=[pl.BlockSpec((DEC_ROWS, hv), lambda i: (i, 0)), st_spec],
        out_shape=[jax.ShapeDtypeStruct((m, hv), F32),
                   jax.ShapeDtypeStruct(states.shape, F32)],
        scratch_shapes=[pltpu.VMEM((DEC_ROWS, hv), F32)],
        input_output_aliases=aliases,
        compiler_params=_cparams("parallel"),
        name="gla_dec",
    )(*args)


def _group_norm_gate(y, z, gn, groups):
    w = y.shape[1] // groups
    yz = y * _silu(z)
    outs = []
    for g in range(groups):
        t = yz[:, g * w:(g + 1) * w]
        ms = jnp.mean(t * t, axis=-1, keepdims=True)
        outs.append(t * lax.rsqrt(ms + EPS) * gn[:, g * w:(g + 1) * w])
    return outs


def _ssd_kernel(xbc_ref, small_ref, dtt_ref, z_ref, cw_ref, cb_ref, dtb_ref, dtbc_ref, al_ref, alc_ref, dx_ref,
                gn_ref, conv0_ref, s0_ref, o_ref, st_ref, cv_ref, s_ref, up_ref, y_ref, *, tile, heads, width):
    c = pl.program_id(1)
    taps = SSM_CONV
    base = 8

    @pl.when(c == 0)
    def _():
        s_ref[...] = s0_ref[0]
        up_ref[base - (taps - 1):base, :] = conv0_ref[0]

    up_ref[base:base + tile, :] = xbc_ref[...]
    conv = cb_ref[...]
    for j in range(taps):
        conv = conv + cw_ref[j:j + 1, :] * up_ref[base - (taps - 1) + j:base - (taps - 1) + j + tile, :]
    tail = up_ref[base + tile - (taps - 1):base + tile, :]
    up_ref[base - (taps - 1):base, :] = tail

    @pl.when(c == pl.num_programs(1) - 1)
    def _():
        cv_ref[0] = tail

    xbc = _silu(conv)
    ns = SSM_D_STATE
    xs = xbc[:, :width]
    bm = xbc[:, width:width + SSM_GROUPS * ns].astype(BF16)
    cm = xbc[:, width + SSM_GROUPS * ns:].astype(BF16)
    dt = _softplus(small_ref[:, SMALL_DT:SMALL_DT + heads] + dtb_ref[...])
    a = -jnp.exp(al_ref[...]) * dt
    tri = _tri_rows(tile, tile)
    acs = _cumsum_rows(a, tri)
    dtt = _softplus(dtt_ref[0] + dtbc_ref[...])
    acs_t = _cumsum_lanes(-jnp.exp(alc_ref[...]) * dtt, _tri_cols(tile))
    a_last = acs[tile - 1:tile, :]
    e_cs = jnp.exp(acs)
    dec = jnp.exp(a_last - acs)
    e_last = jnp.exp(a_last)
    ii = lax.broadcasted_iota(jnp.int32, (tile, tile), 0)
    jj = lax.broadcasted_iota(jnp.int32, (tile, tile), 1)
    low = jj <= ii
    hp = SSM_HEAD_DIM
    per_group = heads // SSM_GROUPS
    scores = [_dot_nt(cm[:, g * ns:(g + 1) * ns], bm[:, g * ns:(g + 1) * ns]) for g in range(SSM_GROUPS)]
    for h in range(heads):
        g = h // per_group
        col = jnp.broadcast_to(acs[:, h:h + 1], (tile, tile))
        row = jnp.broadcast_to(acs_t[h:h + 1, :], (tile, tile))
        lm = jnp.where(low, jnp.exp(jnp.where(low, col - row, 0.0)), 0.0)
        xh = xs[:, h * hp:(h + 1) * hp]
        xdt = xh * jnp.broadcast_to(dt[:, h:h + 1], (tile, hp))
        y = jnp.dot((scores[g] * lm).astype(BF16), xdt.astype(BF16), preferred_element_type=F32)
        sh = s_ref[h]
        y = y + (_dot_nt(cm[:, g * ns:(g + 1) * ns], sh.astype(BF16))
                 * jnp.broadcast_to(e_cs[:, h:h + 1], (tile, hp)))
        st = _dot_tn((xdt * jnp.broadcast_to(dec[:, h:h + 1], (tile, hp))).astype(BF16),
                     bm[:, g * ns:(g + 1) * ns])
        s_ref[h] = jnp.broadcast_to(e_last[:, h:h + 1], (hp, ns)) * sh + st
        y_ref[:, h * hp:(h + 1) * hp] = y + dx_ref[:, h * hp:(h + 1) * hp] * xh
    outs = _group_norm_gate(y_ref[...], z_ref[...], gn_ref[...], SSM_GROUPS)
    gw = width // SSM_GROUPS
    for g in range(SSM_GROUPS):
        o_ref[:, g * gw:(g + 1) * gw] = outs[g].astype(o_ref.dtype)

    @pl.when(c == pl.num_programs(1) - 1)
    def _():
        st_ref[0] = s_ref[...]


def _ssd_prompt(p, small, dtt, blk, sw, conv0, s0, batch, seq):
    m = p.shape[0]
    heads = sw["dtb"].shape[1]
    width = heads * SSM_HEAD_DIM
    ch = sw["cw"].shape[1]
    tile = min(SSD_TILE, seq)
    nb = seq // tile
    full = lambda a: pl.BlockSpec(a.shape, lambda b, c: (0,) * a.ndim)
    return pl.pallas_call(
        functools.partial(_ssd_kernel, tile=tile, heads=heads, width=width),
        grid=(batch, nb),
        in_specs=[pl.BlockSpec((tile, ch), lambda b, c: (b * nb + c, blk["xbc"])),
                  pl.BlockSpec((tile, SMALL_W), lambda b, c: (b * nb + c, 0)),
                  pl.BlockSpec((1, heads, tile), lambda b, c: (b, 0, c)),
                  pl.BlockSpec((tile, width), lambda b, c: (b * nb + c, blk["z"])),
                  full(sw["cw"]), full(sw["cb"]), full(sw["dtb"]), full(sw["dtbc"]), full(sw["al"]),
                  full(sw["alc"]), full(sw["dx"]), full(sw["gn"]),
                  pl.BlockSpec((1, SSM_CONV - 1, ch), lambda b, c: (b, 0, 0)),
                  pl.BlockSpec((1, heads, SSM_HEAD_DIM, SSM_D_STATE), lambda b, c: (b, 0, 0, 0))],
        out_specs=[pl.BlockSpec((tile, width), lambda b, c: (b * nb + c, 0)),
                   pl.BlockSpec((1, heads, SSM_HEAD_DIM, SSM_D_STATE), lambda b, c: (b, 0, 0, 0)),
                   pl.BlockSpec((1, SSM_CONV - 1, ch), lambda b, c: (b, 0, 0))],
        out_shape=[jax.ShapeDtypeStruct((m, width), BF16),
                   jax.ShapeDtypeStruct(s0.shape, F32),
                   jax.ShapeDtypeStruct(conv0.shape, F32)],
        scratch_shapes=[pltpu.VMEM((heads, SSM_HEAD_DIM, SSM_D_STATE), F32),
                        pltpu.VMEM((8 + tile, ch), F32),
                        pltpu.VMEM((tile, width), F32)],
        compiler_params=_cparams("parallel", "arbitrary"),
        name="ssd_prompt",
    )(p, small, dtt, p, sw["cw"], sw["cb"], sw["dtb"], sw["dtbc"], sw["al"], sw["alc"], sw["dx"], sw["gn"],
      conv0, s0)


def _ssd_dec_kernel(xbc_ref, dtx_ref, z_ref, cw_ref, cb_ref, dtbx_ref, alx_ref, dx_ref, gn_ref, conv0_ref, s0_ref,
                    *rest, heads, width):
    o_ref, st_ref, cv_ref, y_ref = rest[-4:]
    taps = SSM_CONV
    ns = SSM_D_STATE
    xr = xbc_ref[...]
    conv = cb_ref[...] + cw_ref[taps - 1:taps, :] * xr
    for j in range(taps - 1):
        conv = conv + cw_ref[j:j + 1, :] * conv0_ref[j]
    for j in range(taps - 2):
        cv_ref[j] = conv0_ref[j + 1]
    cv_ref[taps - 2] = xr
    xbc = _silu(conv)
    xs = xbc[:, :width]
    bm = xbc[:, width:width + SSM_GROUPS * ns]
    cm = xbc[:, width + SSM_GROUPS * ns:].astype(BF16)
    dt = _softplus(dtx_ref[...] + dtbx_ref[...])
    da = jnp.exp(-jnp.exp(alx_ref[...]) * dt)
    xdt = xs * dt
    per_group = width // SSM_GROUPS
    for r in range(DEC_ROWS):
        for cidx in range(width // LANES):
            ls = slice(cidx * LANES, (cidx + 1) * LANES)
            g = (cidx * LANES) // per_group
            hs = cidx * (LANES // SSM_HEAD_DIM)
            dcol = _col_bcast(da[r:r + 1, ls])
            xcol = _col_bcast(xdt[r:r + 1, ls])
            parts = []
            for t in range(LANES // SSM_HEAD_DIM):
                sl = slice(t * SSM_HEAD_DIM, (t + 1) * SSM_HEAD_DIM)
                s_new = dcol[sl, :] * s0_ref[r, hs + t] + xcol[sl, :] * bm[r:r + 1, g * ns:(g + 1) * ns]
                st_ref[r, hs + t] = s_new
                parts.append(s_new)
            s_cat = jnp.concatenate(parts, axis=0).astype(BF16)
            crow = jnp.broadcast_to(cm[r:r + 1, g * ns:(g + 1) * ns], (8, ns))
            y_ref[r:r + 1, ls] = _dot_nt(crow, s_cat)[0:1, :]
    y = y_ref[...] + dx_ref[...] * xs
    outs = _group_norm_gate(y, z_ref[...], gn_ref[...], SSM_GROUPS)
    gw = width // SSM_GROUPS
    for g in range(SSM_GROUPS):
        o_ref[:, g * gw:(g + 1) * gw] = outs[g]


def _ssd_dec(p, dtx, blk, sw, conv_t, states, prev_new, layer):
    m = p.shape[0]
    heads = sw["dtb"].shape[1]
    width = heads * SSM_HEAD_DIM
    ch = sw["cw"].shape[1]
    full = lambda a: pl.BlockSpec(a.shape, lambda i: (0,) * a.ndim)
    st_spec = pl.BlockSpec((None, DEC_ROWS, heads, SSM_HEAD_DIM, SSM_D_STATE), lambda i: (layer, i, 0, 0, 0))
    cv_spec = pl.BlockSpec((None, SSM_CONV - 1, DEC_ROWS, ch), lambda i: (layer, 0, i, 0))
    in_specs = [pl.BlockSpec((DEC_ROWS, ch), lambda i: (i, blk["xbc"])),
                pl.BlockSpec((DEC_ROWS, width), lambda i: (i, 0)),
                pl.BlockSpec((DEC_ROWS, width), lambda i: (i, blk["z"])),
                full(sw["cw"]), full(sw["cb"]), full(sw["dtbx"]), full(sw["alx"]), full(sw["dx"]), full(sw["gn"]),
                cv_spec, st_spec]
    args = [p, dtx, p, sw["cw"], sw["cb"], sw["dtbx"], sw["alx"], sw["dx"], sw["gn"], conv_t, states]
    aliases = {}
    if prev_new is not None:
        in_specs += [pl.BlockSpec(memory_space=pl.ANY)] * 2
        aliases = {len(args): 1, len(args) + 1: 2}
        args += list(prev_new)
    return pl.pallas_call(
        functools.partial(_ssd_dec_kernel, heads=heads, width=width),
        grid=(m // DEC_ROWS,),
        in_specs=in_specs,
        out_specs=[pl.BlockSpec((DEC_ROWS, width), lambda i: (i, 0)), st_spec, cv_spec],
        out_shape=[jax.ShapeDtypeStruct((m, width), F32),
                   jax.ShapeDtypeStruct(states.shape, F32),
                   jax.ShapeDtypeStruct(conv_t.shape, F32)],
        scratch_shapes=[pltpu.VMEM((DEC_ROWS, width), F32)],
        input_output_aliases=aliases,
        compiler_params=_cparams("parallel"),
        name="ssd_dec",
    )(*args)


def _dec_q_kernel(q_ref, wk_ref, tc_ref, ta_ref, tb_ref, o_ref, *, kvl):
    q = q_ref[...]
    qr = _rope128(q[:, MLA_NOPE:], tc_ref[...], ta_ref[...], tb_ref[...])
    o_ref[:, :kvl] = _dot_nt(q[:, :MLA_NOPE].astype(BF16), wk_ref[...])
    o_ref[:, kvl:] = qr


def _dec_q(q, wk, tabs, heads):
    m = q.shape[0]
    kvl = wk.shape[0]
    tab_spec = pl.BlockSpec((m, LANES), lambda h: (0, 0))
    return pl.pallas_call(
        functools.partial(_dec_q_kernel, kvl=kvl),
        grid=(heads,),
        in_specs=[pl.BlockSpec((m, MLA_QPAD), lambda h: (0, h)),
                  pl.BlockSpec((kvl, MLA_NOPE), lambda h: (0, h)),
                  tab_spec, tab_spec, tab_spec],
        out_specs=pl.BlockSpec((m, kvl + LANES), lambda h: (0, h)),
        out_shape=jax.ShapeDtypeStruct((m, heads * (kvl + LANES)), F32),
        compiler_params=_cparams("parallel"),
        name="dec_q",
    )(q, wk, *tabs)


def _dec_attn_kernel(pt_ref, q_ref, lat_new_ref, kr_new_ref, lat_hbm, kr_hbm, o_ref, lat_buf, kr_buf, sem_lat, sem_kr,
                     *, layer, pg, groups, kvl, scale):
    b = pl.program_id(0)
    total = pl.num_programs(0) * groups

    def page_copies(slot, j, page):
        return (pltpu.make_async_copy(lat_hbm.at[layer, page], lat_buf.at[slot, j], sem_lat.at[slot]),
                pltpu.make_async_copy(kr_hbm.at[layer, page], kr_buf.at[slot, j], sem_kr.at[slot]))

    def start_pages(t, slot, j0, j1):
        row = t // groups
        col0 = (t % groups) * pg
        for j in range(j0, j1):
            for c in page_copies(slot, j, pt_ref[row, col0 + j]):
                c.start()

    def wait_group(slot):
        for j in range(pg):
            for c in page_copies(slot, j, 0):
                c.wait()

    @pl.when(b == 0)
    def _():
        for t0 in range(DEC_AHEAD):
            start_pages(t0, t0, 0, pg)

    q = q_ref[0]
    q_lat = q[:, :kvl]
    q_rope = q[:, kvl:kvl + MLA_ROPE]
    r = b % 8
    lat_new = lat_new_ref[pl.ds(r, 1), :]
    kr_new = kr_new_ref[pl.ds(r, 1), :]
    m0 = (jnp.sum(q_lat * lat_new, axis=1, keepdims=True)
          + jnp.sum(q_rope * kr_new, axis=1, keepdims=True)) * scale
    l0 = jnp.ones_like(m0)
    acc0 = jnp.broadcast_to(lat_new, (q.shape[0], kvl))
    qlb = q_lat.astype(BF16)
    qrb = q_rope.astype(BF16)

    def body(g, carry):
        m_prev, l_prev, acc = carry
        t = b * groups + g
        slot = t % DEC_SLOTS
        t_next = jnp.minimum(t + DEC_AHEAD, total - 1)
        s_next = (t + DEC_AHEAD) % DEC_SLOTS
        cut1, cut2 = pg // 3, (2 * pg) // 3
        start_pages(t_next, s_next, 0, cut1)
        wait_group(slot)
        lats, ss = [], []
        for j in range(pg):
            lat = lat_buf[slot, j].astype(BF16)
            kr_t = kr_buf[slot, j].astype(BF16)
            lats.append(lat)
            ss.append((_dot_nt(qlb, lat) + jnp.dot(qrb, kr_t, preferred_element_type=F32)) * scale)
        start_pages(t_next, s_next, cut1, cut2)
        m_new = m_prev
        for s in ss:
            m_new = jnp.maximum(m_new, jnp.max(s, axis=1, keepdims=True))
        corr = jnp.exp(m_prev - m_new)
        l_new = l_prev * corr
        acc = acc * corr
        for j in range(pg):
            pe = jnp.exp(ss[j] - m_new)
            l_new = l_new + jnp.sum(pe, axis=1, keepdims=True)
            acc = acc + jnp.dot(pe.astype(BF16), lats[j], preferred_element_type=F32)
        start_pages(t_next, s_next, cut2, pg)
        return m_new, l_new, acc

    _, l_fin, acc_fin = lax.fori_loop(0, groups, body, (m0, l0, acc0))
    o_ref[0] = acc_fin / l_fin

    @pl.when(b == pl.num_programs(0) - 1)
    def _():
        for extra in range(DEC_AHEAD):
            wait_group((total + extra) % DEC_SLOTS)


def _dec_attn(page_table, q3, lat_new, kr_new, cache_lat, cache_kr_t, layer):
    m, heads, qw = q3.shape
    kvl = cache_lat.shape[-1]
    n_pages = page_table.shape[1]
    pg = min(DEC_PAGES_PER_STEP, n_pages)
    groups = n_pages // pg
    assert groups * pg == n_pages
    scale = (MLA_NOPE + MLA_ROPE) ** -0.5
    return pl.pallas_call(
        functools.partial(_dec_attn_kernel, layer=layer, pg=pg, groups=groups, kvl=kvl, scale=scale),
        grid_spec=pltpu.PrefetchScalarGridSpec(
            num_scalar_prefetch=1,
            grid=(m,),
            in_specs=[pl.BlockSpec((1, heads, qw), lambda b, pt: (b, 0, 0)),
                      pl.BlockSpec((8, kvl), lambda b, pt: (b // 8, 0)),
                      pl.BlockSpec((8, MLA_ROPE), lambda b, pt: (b // 8, 0)),
                      pl.BlockSpec(memory_space=pl.ANY),
                      pl.BlockSpec(memory_space=pl.ANY)],
            out_specs=pl.BlockSpec((1, heads, kvl), lambda b, pt: (b, 0, 0)),
            scratch_shapes=[pltpu.VMEM((DEC_SLOTS, pg, PAGE_SIZE, kvl), F32),
                            pltpu.VMEM((DEC_SLOTS, pg, MLA_ROPE, PAGE_SIZE), F32),
                            pltpu.SemaphoreType.DMA((DEC_SLOTS,)),
                            pltpu.SemaphoreType.DMA((DEC_SLOTS,))]),
        out_shape=jax.ShapeDtypeStruct((m, heads, kvl), F32),
        compiler_params=_cparams("arbitrary"),
        name="dec_attn",
    )(page_table, q3, lat_new, kr_new, cache_lat, cache_kr_t)


def _dec_ov_kernel(o_ref, wv_ref, gate_ref, out_ref):
    out_ref[...] = (jnp.dot(o_ref[...].astype(BF16), wv_ref[...], preferred_element_type=F32)
                    * _silu(gate_ref[...]))


def _dec_ov(o_lat2, wv, p, gate_blk0, heads):
    m = o_lat2.shape[0]
    kvl = wv.shape[0]
    return pl.pallas_call(
        _dec_ov_kernel,
        grid=(heads,),
        in_specs=[pl.BlockSpec((m, kvl), lambda h: (0, h)),
                  pl.BlockSpec((kvl, MLA_V), lambda h: (0, h)),
                  pl.BlockSpec((m, MLA_V), lambda h: (0, gate_blk0 + h))],
        out_specs=pl.BlockSpec((m, MLA_V), lambda h: (0, h)),
        out_shape=jax.ShapeDtypeStruct((m, heads * MLA_V), F32),
        compiler_params=_cparams("parallel"),
        name="dec_ov",
    )(o_lat2, wv, p)


def _post_kernel(y_ref, h_ref, g_ref, o_ref):
    y = y_ref[...]
    ms = jnp.mean(y * y, axis=-1, keepdims=True)
    o_ref[...] = h_ref[...] + y * lax.rsqrt(ms + EPS) * g_ref[...]


def _post(y, h, g, bm):
    m, d = y.shape
    return pl.pallas_call(
        _post_kernel,
        grid=(m // bm,),
        in_specs=[pl.BlockSpec((bm, d), lambda i: (i, 0)),
                  pl.BlockSpec((bm, d), lambda i: (i, 0)),
                  pl.BlockSpec((1, d), lambda i: (0, 0))],
        out_specs=pl.BlockSpec((bm, d), lambda i: (i, 0)),
        out_shape=jax.ShapeDtypeStruct((m, d), F32),
        compiler_params=_cparams("parallel"),
        name="post_norm",
    )(y, h, g)


def _layout(segments):
    off, offs = 0, {}
    for name, width, block in segments:
        off = -(-off // block) * block
        offs[name] = off
        off += width
    return offs, -(-off // LANES) * LANES


def _rope_tables(pos):
    half = MLA_ROPE // 2
    inv_freq = ROPE_THETA ** (-jnp.arange(half, dtype=F32) / half)
    ang = pos.astype(F32)[:, None] * inv_freq[None, :]
    cos, sin = jnp.cos(ang), jnp.sin(ang)
    zeros = jnp.zeros_like(cos)
    pad = jnp.zeros((pos.shape[0], LANES - MLA_ROPE), F32)
    tc = jnp.concatenate([cos, cos, pad], axis=1)
    ta = jnp.concatenate([-sin, zeros, pad], axis=1)
    tb = jnp.concatenate([zeros, sin, pad], axis=1)
    return tc, ta, tb


def _pick(n, prefs):
    for b in prefs:
        if n % b == 0:
            return b
    return n


def kernel(x_prompt, x_sample, cache_mla_latent, cache_mla_krope, state_gla, state_ssm, state_conv, page_table,
           norm_pre, norm_post, w_in, mla_q_norm, mla_w_qb, mla_kv_norm, mla_w_kb, mla_w_vb, gla_w_a2, gla_b_a,
           gla_norm, ssm_conv_w, ssm_conv_b, ssm_dt_bias, ssm_a_log, ssm_d, ssm_norm, w_out):
    depth = w_in.shape[0]
    bp, lp, d = x_prompt.shape
    bs, ls, _ = x_sample.shape
    assert ls == 1
    q_lora = mla_q_norm.shape[1]
    kvl = mla_kv_norm.shape[1]
    heads = mla_w_kb.shape[2]
    dv = gla_norm.shape[1]
    dk = gla_w_a2.shape[2] // GLA_HEADS
    s_heads = ssm_d.shape[1]
    s_width = s_heads * SSM_HEAD_DIM
    conv_ch = ssm_conv_w.shape[2]
    mla_w = heads * MLA_V
    gla_w = GLA_HEADS * dv
    assert dk == LANES and dv % LANES == 0 and kvl % LANES == 0

    in_sizes = (q_lora, kvl + MLA_ROPE, GLA_HEADS * dk, GLA_HEADS * dk, gla_w, gla_w_a2.shape[1], conv_ch, s_heads,
                mla_w, gla_w, s_width)
    names = ("q_a", "kv_a", "g_q", "g_k", "g_v", "g_a", "xbc", "dt", "gate_mla", "gate_gla", "z")
    src, acc = {}, 0
    for nm, sz in zip(names, in_sizes):
        src[nm] = (acc, sz)
        acc += sz
    segs = [("q_a", q_lora, q_lora), ("z", s_width, s_width), ("g_v", gla_w, gla_w), ("gate_gla", gla_w, gla_w),
            ("gate_mla", mla_w, MLA_V), ("xbc", conv_ch, conv_ch), ("ckv", kvl, kvl),
            ("g_q", GLA_HEADS * dk, GLA_HEADS * dk), ("g_k", GLA_HEADS * dk, GLA_HEADS * dk)]
    offs, n_main = _layout(segs)
    blk = {"q_a": offs["q_a"] // q_lora, "z": offs["z"] // s_width, "g_v": offs["g_v"] // dv,
           "gate_gla": offs["gate_gla"] // dv, "gate_mla": offs["gate_mla"] // MLA_V,
           "xbc": offs["xbc"] // conv_ch, "ckv": offs["ckv"] // kvl, "g_q": offs["g_q"] // dk,
           "g_k": offs["g_k"] // dk}

    def w_in_main(wl):
        cols = jnp.zeros((d, n_main), BF16)
        for nm, width, _ in segs:
            s0 = src["kv_a"][0] if nm == "ckv" else src[nm][0]
            cols = lax.dynamic_update_slice(cols, wl[:, s0:s0 + width].astype(BF16), (0, offs[nm]))
        return cols

    def w_in_small(wl):
        kr0 = src["kv_a"][0] + kvl
        parts = [wl[:, kr0:kr0 + MLA_ROPE], wl[:, src["g_a"][0]:src["g_a"][0] + src["g_a"][1]],
                 wl[:, src["dt"][0]:src["dt"][0] + s_heads]]
        assert src["g_a"][1] == SMALL_DT - SMALL_GA and SMALL_DT + s_heads <= SMALL_W
        used = MLA_ROPE + src["g_a"][1] + s_heads
        parts.append(jnp.zeros((d, SMALL_W - used), F32))
        return jnp.concatenate(parts, axis=1).astype(BF16)

    mp, ms_ = bp * lp, bs
    n_pages = page_table.shape[1]
    tabs_p = _rope_tables(jnp.arange(lp))
    tabs_s = _rope_tables(jnp.full((ms_,), n_pages * PAGE_SIZE))
    hp = x_prompt.reshape(mp, d)
    hs = x_sample.reshape(ms_, d)
    bn_main = _pick(n_main, (1024, 512, 256, 128))
    bn_out = _pick(d, (1024, 512, 256, 128))
    bm_p = _pick(mp, (1024, 512, 256))
    bm_row = _pick(mp, (256,))
    bm_kv = _pick(lp, (256, 128))
    conv0_p = jnp.zeros((bp, SSM_CONV - 1, conv_ch), F32)
    ssm0_p = jnp.zeros((bp, s_heads, SSM_HEAD_DIM, SSM_D_STATE), F32)
    cache_kr_t = jnp.swapaxes(cache_mla_krope, 2, 3)
    conv_t = jnp.swapaxes(state_conv, 1, 2)
    gla_new_s = jnp.zeros(state_gla.shape, F32)
    ssm_new_s = jnp.zeros(state_ssm.shape, F32)
    conv_new_s = jnp.zeros(conv_t.shape, F32)

    rows_p, rows_s = [], []
    for l in range(depth):
        wl = w_in[l]
        w_main = w_in_main(wl)
        w_small = w_in_small(wl)
        w_qb = jnp.pad(mla_w_qb[l].reshape(q_lora, heads, MLA_NOPE + MLA_ROPE),
                       ((0, 0), (0, 0), (0, MLA_QPAD - MLA_NOPE - MLA_ROPE))).reshape(q_lora, heads * MLA_QPAD)
        w_qb = w_qb.astype(BF16)
        w_kb = mla_w_kb[l].reshape(kvl, heads * MLA_NOPE).astype(BF16)
        w_vb = mla_w_vb[l].reshape(kvl, heads * MLA_V).astype(BF16)
        w_o = w_out[l].astype(BF16)
        wa_pad = jnp.zeros((SMALL_W, GLA_HEADS * dk), F32)
        wa_pad = lax.dynamic_update_slice(wa_pad, gla_w_a2[l], (SMALL_GA, 0)).astype(BF16)
        ba = gla_b_a[l][None, :]
        gn_gla = gla_norm[l][None, :]
        sw = {"cw": ssm_conv_w[l], "cb": ssm_conv_b[l][None, :], "dtb": ssm_dt_bias[l][None, :],
              "dtbc": ssm_dt_bias[l][:, None], "al": ssm_a_log[l][None, :], "alc": ssm_a_log[l][:, None],
              "dx": jnp.repeat(ssm_d[l], SSM_HEAD_DIM)[None, :], "gn": ssm_norm[l][None, :],
              "dtbx": jnp.repeat(ssm_dt_bias[l], SSM_HEAD_DIM)[None, :],
              "alx": jnp.repeat(ssm_a_log[l], SSM_HEAD_DIM)[None, :]}
        g_pre = norm_pre[l][None, :]
        g_post = norm_post[l][None, :]
        g_q = mla_q_norm[l][None, :]
        g_kv = mla_kv_norm[l][None, :]

        u, small = _norm_small(hp, g_pre, w_small, bm_row)
        p = _matmul([u], w_main, [d], bm_p, bn_main)
        q = _norm_matmul(p, blk["q_a"], g_q, w_qb, _pick(mp, (512, 256)), _pick(heads * MLA_QPAD, (1024, 512, 256)))
        lat, kr, kcat, v = _mla_kv(p, blk["ckv"], small, g_kv, tabs_p, w_kb, w_vb, bm_kv, True)
        o_mla = _flash(q, kcat, v, p, blk["gate_mla"], tabs_p, bp, lp, heads)
        o_gla, gla_new = _gla_prompt(p, small, blk, wa_pad, ba, gn_gla, bp, lp, dk, dv)
        dtt = small[:, SMALL_DT:SMALL_DT + s_heads].reshape(bp, lp, s_heads).transpose(0, 2, 1)
        o_ssm, ssm_new, conv_new = _ssd_prompt(p, small, dtt, blk, sw, conv0_p, ssm0_p, bp, lp)
        y = _matmul([o_mla, o_gla, o_ssm], w_o, [mla_w, gla_w, s_width], bm_p, bn_out)
        hp = _post(y, hp, g_post, bm_row)
        rows_p.append((lat.reshape(bp, lp, kvl), kr.reshape(bp, lp, MLA_ROPE), gla_new, ssm_new, conv_new))

        u, small = _norm_small(hs, g_pre, w_small, ms_)
        p = _matmul([u], w_main, [d], ms_, bn_main)
        q = _norm_matmul(p, blk["q_a"], g_q, w_qb, ms_, _pick(heads * MLA_QPAD, (1024, 512, 256)))
        lat, kr = _mla_kv(p, blk["ckv"], small, g_kv, tabs_s, w_kb, w_vb, ms_, False)
        qcat = _dec_q(q, w_kb, tabs_s, heads).reshape(ms_, heads, kvl + LANES)
        o_lat = _dec_attn(page_table, qcat, lat, kr, cache_mla_latent, cache_kr_t, l)
        o_mla = _dec_ov(o_lat.reshape(ms_, heads * kvl), w_vb, p, blk["gate_mla"], heads)
        o_gla, gla_new_s = _gla_dec(p, small, blk, wa_pad, ba, gn_gla, state_gla, gla_new_s, l, dk, dv)
        dtx = jnp.repeat(small[:, SMALL_DT:SMALL_DT + s_heads], SSM_HEAD_DIM, axis=1)
        o_ssm, ssm_new_s, conv_new_s = _ssd_dec(p, dtx, blk, sw, conv_t, state_ssm, (ssm_new_s, conv_new_s), l)
        y = _matmul([o_mla, o_gla, o_ssm], w_o, [mla_w, gla_w, s_width], ms_, bn_out)
        hs = _post(y, hs, g_post, ms_)
        rows_s.append((lat.reshape(bs, 1, kvl), kr.reshape(bs, 1, MLA_ROPE)))

    outs = [hp.reshape(bp, lp, d), hs.reshape(bs, 1, d)]
    for i in range(5):
        outs.append(jnp.stack([r[i] for r in rows_p]))
    for i in range(2):
        outs.append(jnp.stack([r[i] for r in rows_s]))
    outs += [gla_new_s, ssm_new_s, jnp.swapaxes(conv_new_s, 1, 2)]
    return tuple(outs)
```

```python
import functools
import math

import jax
import jax.numpy as jnp
from jax import lax
from jax.experimental import pallas as pl
from jax.experimental.pallas import tpu as pltpu

F32 = jnp.float32
BF16 = jnp.bfloat16

EPS = 1e-6
MLA_V = 128
MLA_NOPE = 128
MLA_ROPE = 64
MLA_QPAD = 256
ROPE_THETA = 10000.0
GLA_HEADS = 4
GLA_GATE_NORM = 16.0
GLA_CHUNK = 32
SSM_HEAD_DIM = 64
SSM_GROUPS = 2
SSM_D_STATE = 128
SSM_CONV = 4
PAGE_SIZE = 128
LANES = 128
SMALL_W = 128
SMALL_KROPE = 0
SMALL_GA = 64
SMALL_DT = 80

GLA_TILE = 256
GLA_HEADS_PER_STEP = 4
SSD_TILE = 256
FLASH_BQ = 512
DEC_PAGES_PER_STEP = 16
DEC_SLOTS = 3
DEC_AHEAD = DEC_SLOTS - 1
DEC_ROWS = 8
VMEM_LIMIT = 52 * 1024 * 1024


def _cparams(*sem):
    return pltpu.CompilerParams(dimension_semantics=sem, vmem_limit_bytes=VMEM_LIMIT)


def _silu(x):
    return x * (1.0 / (1.0 + jnp.exp(-x)))


def _softplus(x):
    return jnp.maximum(x, 0.0) + jnp.log1p(jnp.exp(-jnp.abs(x)))


def _split3(x):
    a = x.astype(BF16)
    r = x - a.astype(F32)
    b = r.astype(BF16)
    c = (r - b.astype(F32)).astype(BF16)
    return a, b, c


def _tri_rows(t, block):
    i = lax.broadcasted_iota(jnp.int32, (t, t), 0)
    j = lax.broadcasted_iota(jnp.int32, (t, t), 1)
    return ((j <= i) & ((i // block) == (j // block))).astype(BF16)


def _tri_cols(t):
    i = lax.broadcasted_iota(jnp.int32, (t, t), 0)
    j = lax.broadcasted_iota(jnp.int32, (t, t), 1)
    return (i <= j).astype(BF16)


def _cumsum_rows(x, tri):
    a, b, c = _split3(x)
    d = functools.partial(jnp.dot, preferred_element_type=F32)
    return d(tri, a) + d(tri, b) + d(tri, c)


def _cumsum_lanes(x, tri_t):
    a, b, c = _split3(x)
    d = functools.partial(jnp.dot, preferred_element_type=F32)
    return d(a, tri_t) + d(b, tri_t) + d(c, tri_t)


def _dot_nt(a, b):
    return lax.dot_general(a, b, (((1,), (1,)), ((), ())), preferred_element_type=F32)


def _dot_tn(a, b):
    return lax.dot_general(a, b, (((0,), (0,)), ((), ())), preferred_element_type=F32)


def _col_bcast(row):
    return jnp.broadcast_to(row, (LANES, LANES)).T


def _rope128(x, tc, ta, tb):
    return x * tc + pltpu.roll(x, 96, axis=1) * ta + pltpu.roll(x, 32, axis=1) * tb


def _norm_small_kernel(x_ref, g_ref, ws_ref, u_ref, s_ref):
    x = x_ref[...]
    ms = jnp.mean(x * x, axis=-1, keepdims=True)
    u = (x * lax.rsqrt(ms + EPS) * g_ref[...]).astype(BF16)
    u_ref[...] = u
    s_ref[...] = _dot_nt(u, ws_ref[...])


def _norm_small(x, g, w_small, bm):
    m, d = x.shape
    return pl.pallas_call(
        _norm_small_kernel,
        grid=(m // bm,),
        in_specs=[pl.BlockSpec((bm, d), lambda i: (i, 0)),
                  pl.BlockSpec((1, d), lambda i: (0, 0)),
                  pl.BlockSpec((SMALL_W, d), lambda i: (0, 0))],
        out_specs=[pl.BlockSpec((bm, d), lambda i: (i, 0)),
                   pl.BlockSpec((bm, SMALL_W), lambda i: (i, 0))],
        out_shape=[jax.ShapeDtypeStruct((m, d), BF16), jax.ShapeDtypeStruct((m, SMALL_W), F32)],
        compiler_params=_cparams("parallel"),
        name="norm_small",
    )(x, g, w_small)


def _mm_kernel(*refs, n):
    o_ref = refs[2 * n]
    acc = None
    for i in range(n):
        p = jnp.dot(refs[i][...].astype(BF16), refs[n + i][...], preferred_element_type=F32)
        acc = p if acc is None else acc + p
    o_ref[...] = acc.astype(o_ref.dtype)


def _matmul(lhs_list, w, k_sizes, bm, bn, out_dtype=F32):
    n = len(lhs_list)
    m = lhs_list[0].shape[0]
    ncols = w.shape[1]
    in_specs, k0 = [], 0
    for a in lhs_list:
        in_specs.append(pl.BlockSpec((bm, a.shape[1]), lambda i, j: (i, 0)))
    for ks in k_sizes:
        assert k0 % ks == 0
        in_specs.append(pl.BlockSpec((ks, bn), functools.partial(lambda i, j, kb: (kb, j), kb=k0 // ks)))
        k0 += ks
    return pl.pallas_call(
        functools.partial(_mm_kernel, n=n),
        grid=(m // bm, ncols // bn),
        in_specs=in_specs,
        out_specs=pl.BlockSpec((bm, bn), lambda i, j: (i, j)),
        out_shape=jax.ShapeDtypeStruct((m, ncols), out_dtype),
        compiler_params=_cparams("parallel", "parallel"),
        name="matmul%d" % n,
    )(*lhs_list, *([w] * n))


def _mm_nt_kernel(a_ref, wt_ref, o_ref):
    o_ref[...] = _dot_nt(a_ref[...], wt_ref[...]).astype(o_ref.dtype)


def _matmul_nt(a, w_t, bm, bn):
    m, k = a.shape
    ncols = w_t.shape[0]
    return pl.pallas_call(
        _mm_nt_kernel,
        grid=(m // bm, ncols // bn),
        in_specs=[pl.BlockSpec((bm, k), lambda i, j: (i, 0)),
                  pl.BlockSpec((bn, k), lambda i, j: (j, 0))],
        out_specs=pl.BlockSpec((bm, bn), lambda i, j: (i, j)),
        out_shape=jax.ShapeDtypeStruct((m, ncols), F32),
        compiler_params=_cparams("parallel", "parallel"),
        name="in_proj",
    )(a, w_t)


def _norm_mm_kernel(x_ref, g_ref, w_ref, o_ref):
    x = x_ref[...]
    ms = jnp.mean(x * x, axis=-1, keepdims=True)
    xn = (x * lax.rsqrt(ms + EPS) * g_ref[...]).astype(BF16)
    o_ref[...] = jnp.dot(xn, w_ref[...], preferred_element_type=F32).astype(o_ref.dtype)


def _norm_matmul(p, col_blk, g, w, bm, bn):
    m = p.shape[0]
    k, ncols = w.shape
    return pl.pallas_call(
        _norm_mm_kernel,
        grid=(m // bm, ncols // bn),
        in_specs=[pl.BlockSpec((bm, k), lambda i, j: (i, col_blk)),
                  pl.BlockSpec((1, k), lambda i, j: (0, 0)),
                  pl.BlockSpec((k, bn), lambda i, j: (0, j))],
        out_specs=pl.BlockSpec((bm, bn), lambda i, j: (i, j)),
        out_shape=jax.ShapeDtypeStruct((m, ncols), BF16),
        compiler_params=_cparams("parallel", "parallel"),
        name="q_proj",
    )(p, g, w)


def _mla_kv_kernel(ckv_ref, small_ref, g_ref, tc_ref, ta_ref, tb_ref, *rest, expand, heads):
    if expand:
        wk_ref, wv_ref, lat_ref, kr_ref, kcat_ref, v_ref = rest
    else:
        lat_ref, kr_ref = rest
    x = ckv_ref[...]
    ms = jnp.mean(x * x, axis=-1, keepdims=True)
    c = x * lax.rsqrt(ms + EPS) * g_ref[...]
    lat_ref[...] = c
    kr = _rope128(small_ref[...], tc_ref[...], ta_ref[...], tb_ref[...])
    kr_ref[...] = kr[:, :MLA_ROPE]
    if expand:
        cb = c.astype(BF16)
        kn = jnp.dot(cb, wk_ref[...], preferred_element_type=F32).astype(BF16)
        v_ref[...] = jnp.dot(cb, wv_ref[...], preferred_element_type=F32).astype(BF16)
        krb = kr.astype(BF16)
        for h in range(heads):
            kcat_ref[:, h * MLA_QPAD:h * MLA_QPAD + MLA_NOPE] = kn[:, h * MLA_NOPE:(h + 1) * MLA_NOPE]
            kcat_ref[:, h * MLA_QPAD + MLA_NOPE:(h + 1) * MLA_QPAD] = krb


def _mla_kv(p, ckv_blk, small, g, tabs, wk, wv, bm, expand):
    m = p.shape[0]
    kvl = g.shape[1]
    nt = tabs[0].shape[0] // bm
    heads = wk.shape[1] // MLA_NOPE
    tab_spec = pl.BlockSpec((bm, LANES), lambda i: (i % nt, 0))
    in_specs = [pl.BlockSpec((bm, kvl), lambda i: (i, ckv_blk)),
                pl.BlockSpec((bm, SMALL_W), lambda i: (i, 0)),
                pl.BlockSpec((1, kvl), lambda i: (0, 0)),
                tab_spec, tab_spec, tab_spec]
    out_specs = [pl.BlockSpec((bm, kvl), lambda i: (i, 0)),
                 pl.BlockSpec((bm, MLA_ROPE), lambda i: (i, 0))]
    out_shape = [jax.ShapeDtypeStruct((m, kvl), F32), jax.ShapeDtypeStruct((m, MLA_ROPE), F32)]
    args = [p, small, g, *tabs]
    if expand:
        in_specs += [pl.BlockSpec(wk.shape, lambda i: (0, 0)), pl.BlockSpec(wv.shape, lambda i: (0, 0))]
        out_specs += [pl.BlockSpec((bm, heads * MLA_QPAD), lambda i: (i, 0)),
                      pl.BlockSpec((bm, heads * MLA_V), lambda i: (i, 0))]
        out_shape += [jax.ShapeDtypeStruct((m, heads * MLA_QPAD), BF16),
                      jax.ShapeDtypeStruct((m, heads * MLA_V), BF16)]
        args += [wk, wv]
    return pl.pallas_call(
        functools.partial(_mla_kv_kernel, expand=expand, heads=heads),
        grid=(m // bm,),
        in_specs=in_specs, out_specs=out_specs, out_shape=out_shape,
        compiler_params=_cparams("parallel"),
        name="mla_kv",
    )(*args)


def _flash_kernel(q_ref, k_ref, v_ref, gate_ref, tc_ref, ta_ref, tb_ref, o_ref, qs_ref, *, seq, bq, scale):
    q = q_ref[...].astype(F32)
    qr = _rope128(q[:, MLA_NOPE:], tc_ref[...], ta_ref[...], tb_ref[...])
    qs_ref[:, :MLA_NOPE] = (q[:, :MLA_NOPE] * scale).astype(BF16)
    qs_ref[:, MLA_NOPE:] = (qr * scale).astype(BF16)
    ii = lax.broadcasted_iota(jnp.int32, (bq, bq), 0)
    jj = lax.broadcasted_iota(jnp.int32, (bq, bq), 1)
    for i in range(seq // bq):
        rows = slice(i * bq, (i + 1) * bq)
        qi = qs_ref[rows, :]
        s_d = jnp.where(jj <= ii, _dot_nt(qi, k_ref[rows, :]), -jnp.inf)
        m = jnp.max(s_d, axis=1, keepdims=True)
        if i > 0:
            s_a = _dot_nt(qi, k_ref[0:i * bq, :])
            m = jnp.maximum(m, jnp.max(s_a, axis=1, keepdims=True))
        p_d = jnp.exp(s_d - m)
        l = jnp.sum(p_d, axis=1, keepdims=True)
        o = jnp.dot(p_d.astype(BF16), v_ref[rows, :], preferred_element_type=F32)
        if i > 0:
            p_a = jnp.exp(s_a - m)
            l = l + jnp.sum(p_a, axis=1, keepdims=True)
            o = o + jnp.dot(p_a.astype(BF16), v_ref[0:i * bq, :], preferred_element_type=F32)
        o_ref[rows, :] = (o * (1.0 / l) * _silu(gate_ref[rows, :])).astype(o_ref.dtype)


def _flash(q, kcat, v, p, gate_blk0, tabs, batch, seq, heads):
    m = q.shape[0]
    bq = min(FLASH_BQ, seq)
    scale = (MLA_NOPE + MLA_ROPE) ** -0.5
    tab_spec = pl.BlockSpec((seq, LANES), lambda b, h: (0, 0))
    return pl.pallas_call(
        functools.partial(_flash_kernel, seq=seq, bq=bq, scale=scale),
        grid=(batch, heads),
        in_specs=[pl.BlockSpec((seq, MLA_QPAD), lambda b, h: (b, h)),
                  pl.BlockSpec((seq, MLA_QPAD), lambda b, h: (b, h)),
                  pl.BlockSpec((seq, MLA_V), lambda b, h: (b, h)),
                  pl.BlockSpec((seq, MLA_V), lambda b, h: (b, gate_blk0 + h)),
                  tab_spec, tab_spec, tab_spec],
        out_specs=pl.BlockSpec((seq, MLA_V), lambda b, h: (b, h)),
        out_shape=jax.ShapeDtypeStruct((m, heads * MLA_V), BF16),
        scratch_shapes=[pltpu.VMEM((seq, MLA_QPAD), BF16)],
        compiler_params=_cparams("parallel", "parallel"),
        name="flash",
    )(q, kcat, v, p, *tabs)


def _log_decay(small, wa_ref, ba_ref):
    x = jnp.dot(small.astype(BF16), wa_ref[...], preferred_element_type=F32) + ba_ref[...]
    return -_softplus(-x) * (1.0 / GLA_GATE_NORM)


def _gla_kernel(q_ref, k_ref, v_ref, small_ref, gate_ref, wa_ref, ba_ref, gn_ref, o_ref, st_ref, s_ref,
                *, tile, dk, dv, hpb):
    c = pl.program_id(2)

    @pl.when(c == 0)
    def _():
        s_ref[...] = jnp.zeros(s_ref.shape, F32)

    la = _log_decay(small_ref[...], wa_ref, ba_ref)
    bcs = _cumsum_rows(la, _tri_rows(tile, GLA_CHUNK))
    ii = lax.broadcasted_iota(jnp.int32, (GLA_CHUNK, GLA_CHUNK), 0)
    jj = lax.broadcasted_iota(jnp.int32, (GLA_CHUNK, GLA_CHUNK), 1)
    scale = dk ** -0.5
    outs = [[] for _ in range(hpb)]
    for i in range(tile // GLA_CHUNK):
        sl = slice(i * GLA_CHUNK, (i + 1) * GLA_CHUNK)
        for h in range(hpb):
            ks = slice(h * dk, (h + 1) * dk)
            bi = bcs[sl, ks]
            ki = k_ref[sl, ks]
            vi = v_ref[sl, h * dv:(h + 1) * dv].astype(BF16)
            qd = (q_ref[sl, ks] * scale * jnp.exp(bi)).astype(BF16)
            kd = (ki * jnp.exp(-bi)).astype(BF16)
            att = jnp.where(jj <= ii, _dot_nt(qd, kd), 0.0).astype(BF16)
            s_old = s_ref[h]
            o = (jnp.dot(qd, s_old.astype(BF16), preferred_element_type=F32)
                 + jnp.dot(att, vi, preferred_element_type=F32))
            outs[h].append(o)
            bl = bi[GLA_CHUNK - 1:GLA_CHUNK, :]
            kv = _dot_tn((ki * jnp.exp(bl - bi)).astype(BF16), vi)
            dcol = _col_bcast(jnp.exp(bl))
            for j in range(dv // LANES):
                ls = slice(j * LANES, (j + 1) * LANES)
                s_ref[h, :, ls] = dcol * s_old[:, ls] + kv[:, ls]
    for h in range(hpb):
        vs = slice(h * dv, (h + 1) * dv)
        o = jnp.concatenate(outs[h], axis=0)
        ms = jnp.mean(o * o, axis=-1, keepdims=True)
        o = o * lax.rsqrt(ms + EPS) * gn_ref[...]
        o_ref[:, vs] = (o * _silu(gate_ref[:, vs])).astype(o_ref.dtype)

    @pl.when(c == pl.num_programs(2) - 1)
    def _():
        st_ref[0] = s_ref[...]


def _gla_prompt(p, small, blk, wa_pad, ba, gn, batch, seq, dk, dv):
    m = p.shape[0]
    tile = min(GLA_TILE, seq)
    nb = seq // tile
    hpb = min(GLA_HEADS_PER_STEP, GLA_HEADS)
    hg = GLA_HEADS // hpb
    wk, wv = hpb * dk, hpb * dv
    row = lambda b, h, c: b * nb + c
    return pl.pallas_call(
        functools.partial(_gla_kernel, tile=tile, dk=dk, dv=dv, hpb=hpb),
        grid=(batch, hg, nb),
        in_specs=[pl.BlockSpec((tile, wk), lambda b, h, c: (row(b, h, c), blk["g_q"] // hpb + h)),
                  pl.BlockSpec((tile, wk), lambda b, h, c: (row(b, h, c), blk["g_k"] // hpb + h)),
                  pl.BlockSpec((tile, wv), lambda b, h, c: (row(b, h, c), blk["g_v"] // hpb + h)),
                  pl.BlockSpec((tile, SMALL_W), lambda b, h, c: (row(b, h, c), 0)),
                  pl.BlockSpec((tile, wv), lambda b, h, c: (row(b, h, c), blk["gate_gla"] // hpb + h)),
                  pl.BlockSpec((SMALL_W, wk), lambda b, h, c: (0, h)),
                  pl.BlockSpec((1, wk), lambda b, h, c: (0, h)),
                  pl.BlockSpec((1, dv), lambda b, h, c: (0, 0))],
        out_specs=[pl.BlockSpec((tile, wv), lambda b, h, c: (row(b, h, c), h)),
                   pl.BlockSpec((1, hpb, dk, dv), lambda b, h, c: (b, h, 0, 0))],
        out_shape=[jax.ShapeDtypeStruct((m, GLA_HEADS * dv), BF16),
                   jax.ShapeDtypeStruct((batch, GLA_HEADS, dk, dv), F32)],
        scratch_shapes=[pltpu.VMEM((hpb, dk, dv), F32)],
        compiler_params=_cparams("parallel", "parallel", "arbitrary"),
        name="gla_prompt",
    )(p, p, p, small, p, wa_pad, ba, gn)


def _gla_dec_kernel(q_ref, k_ref, v_ref, small_ref, gate_ref, wa_ref, ba_ref, gn_ref, s0_ref, *rest, dk, dv):
    o_ref, st_ref, ob_ref = rest[-3:]
    la = _log_decay(small_ref[...], wa_ref, ba_ref)
    a = jnp.exp(la)
    q = q_ref[...] * (dk ** -0.5)
    k = k_ref[...]
    v = v_ref[...]
    for r in range(DEC_ROWS):
        for h in range(GLA_HEADS):
            ks = slice(h * dk, (h + 1) * dk)
            acol = _col_bcast(a[r:r + 1, ks])
            kcol = _col_bcast(k[r:r + 1, ks])
            qcol = _col_bcast(q[r:r + 1, ks])
            for j in range(dv // LANES):
                ls = slice(j * LANES, (j + 1) * LANES)
                vrow = v[r:r + 1, h * dv + j * LANES:h * dv + (j + 1) * LANES]
                s_new = acol * s0_ref[r, h, :, ls] + kcol * vrow
                st_ref[r, h, :, ls] = s_new
                ob_ref[r:r + 1, h * dv + j * LANES:h * dv + (j + 1) * LANES] = jnp.sum(
                    qcol * s_new, axis=0, keepdims=True)
    for h in range(GLA_HEADS):
        vs = slice(h * dv, (h + 1) * dv)
        o = ob_ref[:, vs]
        ms = jnp.mean(o * o, axis=-1, keepdims=True)
        o_ref[:, vs] = o * lax.rsqrt(ms + EPS) * gn_ref[...] * _silu(gate_ref[:, vs])


def _gla_dec(p, small, blk, wa_pad, ba, gn, states, prev_new, layer, dk, dv):
    m = p.shape[0]
    hk, hv = GLA_HEADS * dk, GLA_HEADS * dv
    st_spec = pl.BlockSpec((None, DEC_ROWS, GLA_HEADS, dk, dv), lambda i: (layer, i, 0, 0, 0))
    in_specs = [pl.BlockSpec((DEC_ROWS, hk), lambda i: (i, blk["g_q"] * dk // hk)),
                pl.BlockSpec((DEC_ROWS, hk), lambda i: (i, blk["g_k"] * dk // hk)),
                pl.BlockSpec((DEC_ROWS, hv), lambda i: (i, blk["g_v"] * dv // hv)),
                pl.BlockSpec((DEC_ROWS, SMALL_W), lambda i: (i, 0)),
                pl.BlockSpec((DEC_ROWS, hv), lambda i: (i, blk["gate_gla"] * dv // hv)),
                pl.BlockSpec((SMALL_W, hk), lambda i: (0, 0)),
                pl.BlockSpec((1, hk), lambda i: (0, 0)),
                pl.BlockSpec((1, dv), lambda i: (0, 0)),
                st_spec]
    args = [p, p, p, small, p, wa_pad, ba, gn, states]
    aliases = {}
    if prev_new is not None:
        in_specs.append(pl.BlockSpec(memory_space=pl.ANY))
        aliases = {len(args): 1}
        args.append(prev_new)
    return pl.pallas_call(
        functools.partial(_gla_dec_kernel, dk=dk, dv=dv),
        grid=(m // DEC_ROWS,),
        in_specs=in_specs,
        out_specs=[pl.BlockSpec((DEC_ROWS, hv), lambda i: (i, 0)), st_spec],
        out_shape=[jax.ShapeDtypeStruct((m, hv), F32),
                   jax.ShapeDtypeStruct(states.shape, F32)],
        scratch_shapes=[pltpu.VMEM((DEC_ROWS, hv), F32)],
        input_output_aliases=aliases,
        compiler_params=_cparams("parallel"),
        name="gla_dec",
    )(*args)


def _group_norm_gate(y, z, gn, groups):
    w = y.shape[1] // groups
    yz = y * _silu(z)
    outs = []
    for g in range(groups):
        t = yz[:, g * w:(g + 1) * w]
        ms = jnp.mean(t * t, axis=-1, keepdims=True)
        outs.append(t * lax.rsqrt(ms + EPS) * gn[:, g * w:(g + 1) * w])
    return outs


def _ssd_kernel(xbc_ref, small_ref, dtt_ref, z_ref, cw_ref, cb_ref, dtb_ref, dtbc_ref, al_ref, alc_ref, dx_ref,
                gn_ref, conv0_ref, s0_ref, o_ref, st_ref, cv_ref, s_ref, up_ref, y_ref, *, tile, heads, width):
    c = pl.program_id(1)
    taps = SSM_CONV
    base = 8

    @pl.when(c == 0)
    def _():
        s_ref[...] = s0_ref[0]
        up_ref[base - (taps - 1):base, :] = conv0_ref[0]

    up_ref[base:base + tile, :] = xbc_ref[...]
    conv = cb_ref[...]
    for j in range(taps):
        conv = conv + cw_ref[j:j + 1, :] * up_ref[base - (taps - 1) + j:base - (taps - 1) + j + tile, :]
    tail = up_ref[base + tile - (taps - 1):base + tile, :]
    up_ref[base - (taps - 1):base, :] = tail

    @pl.when(c == pl.num_programs(1) - 1)
    def _():
        cv_ref[0] = tail

    xbc = _silu(conv)
    ns = SSM_D_STATE
    xs = xbc[:, :width]
    bm = xbc[:, width:width + SSM_GROUPS * ns].astype(BF16)
    cm = xbc[:, width + SSM_GROUPS * ns:].astype(BF16)
    dt = _softplus(small_ref[:, SMALL_DT:SMALL_DT + heads] + dtb_ref[...])
    a = -jnp.exp(al_ref[...]) * dt
    tri = _tri_rows(tile, tile)
    acs = _cumsum_rows(a, tri)
    dtt = _softplus(dtt_ref[0] + dtbc_ref[...])
    acs_t = _cumsum_lanes(-jnp.exp(alc_ref[...]) * dtt, _tri_cols(tile))
    a_last = acs[tile - 1:tile, :]
    e_cs = jnp.exp(acs)
    dec = jnp.exp(a_last - acs)
    e_last = jnp.exp(a_last)
    ii = lax.broadcasted_iota(jnp.int32, (tile, tile), 0)
    jj = lax.broadcasted_iota(jnp.int32, (tile, tile), 1)
    low = jj <= ii
    hp = SSM_HEAD_DIM
    per_group = heads // SSM_GROUPS
    scores = [_dot_nt(cm[:, g * ns:(g + 1) * ns], bm[:, g * ns:(g + 1) * ns]) for g in range(SSM_GROUPS)]
    for h in range(heads):
        g = h // per_group
        col = jnp.broadcast_to(acs[:, h:h + 1], (tile, tile))
        row = jnp.broadcast_to(acs_t[h:h + 1, :], (tile, tile))
        lm = jnp.where(low, jnp.exp(jnp.where(low, col - row, 0.0)), 0.0)
        xh = xs[:, h * hp:(h + 1) * hp]
        xdt = xh * jnp.broadcast_to(dt[:, h:h + 1], (tile, hp))
        y = jnp.dot((scores[g] * lm).astype(BF16), xdt.astype(BF16), preferred_element_type=F32)
        sh = s_ref[h]
        y = y + (_dot_nt(cm[:, g * ns:(g + 1) * ns], sh.astype(BF16))
                 * jnp.broadcast_to(e_cs[:, h:h + 1], (tile, hp)))
        st = _dot_tn((xdt * jnp.broadcast_to(dec[:, h:h + 1], (tile, hp))).astype(BF16),
                     bm[:, g * ns:(g + 1) * ns])
        s_ref[h] = jnp.broadcast_to(e_last[:, h:h + 1], (hp, ns)) * sh + st
        y_ref[:, h * hp:(h + 1) * hp] = y + dx_ref[:, h * hp:(h + 1) * hp] * xh
    outs = _group_norm_gate(y_ref[...], z_ref[...], gn_ref[...], SSM_GROUPS)
    gw = width // SSM_GROUPS
    for g in range(SSM_GROUPS):
        o_ref[:, g * gw:(g + 1) * gw] = outs[g].astype(o_ref.dtype)

    @pl.when(c == pl.num_programs(1) - 1)
    def _():
        st_ref[0] = s_ref[...]


def _ssd_prompt(p, small, dtt, blk, sw, conv0, s0, batch, seq):
    m = p.shape[0]
    heads = sw["dtb"].shape[1]
    width = heads * SSM_HEAD_DIM
    ch = sw["cw"].shape[1]
    tile = min(SSD_TILE, seq)
    nb = seq // tile
    full = lambda a: pl.BlockSpec(a.shape, lambda b, c: (0,) * a.ndim)
    return pl.pallas_call(
        functools.partial(_ssd_kernel, tile=tile, heads=heads, width=width),
        grid=(batch, nb),
        in_specs=[pl.BlockSpec((tile, ch), lambda b, c: (b * nb + c, blk["xbc"])),
                  pl.BlockSpec((tile, SMALL_W), lambda b, c: (b * nb + c, 0)),
                  pl.BlockSpec((1, heads, tile), lambda b, c: (b, 0, c)),
                  pl.BlockSpec((tile, width), lambda b, c: (b * nb + c, blk["z"])),
                  full(sw["cw"]), full(sw["cb"]), full(sw["dtb"]), full(sw["dtbc"]), full(sw["al"]),
                  full(sw["alc"]), full(sw["dx"]), full(sw["gn"]),
                  pl.BlockSpec((1, SSM_CONV - 1, ch), lambda b, c: (b, 0, 0)),
                  pl.BlockSpec((1, heads, SSM_HEAD_DIM, SSM_D_STATE), lambda b, c: (b, 0, 0, 0))],
        out_specs=[pl.BlockSpec((tile, width), lambda b, c: (b * nb + c, 0)),
                   pl.BlockSpec((1, heads, SSM_HEAD_DIM, SSM_D_STATE), lambda b, c: (b, 0, 0, 0)),
                   pl.BlockSpec((1, SSM_CONV - 1, ch), lambda b, c: (b, 0, 0))],
        out_shape=[jax.ShapeDtypeStruct((m, width), BF16),
                   jax.ShapeDtypeStruct(s0.shape, F32),
                   jax.ShapeDtypeStruct(conv0.shape, F32)],
        scratch_shapes=[pltpu.VMEM((heads, SSM_HEAD_DIM, SSM_D_STATE), F32),
                        pltpu.VMEM((8 + tile, ch), F32),
                        pltpu.VMEM((tile, width), F32)],
        compiler_params=_cparams("parallel", "arbitrary"),
        name="ssd_prompt",
    )(p, small, dtt, p, sw["cw"], sw["cb"], sw["dtb"], sw["dtbc"], sw["al"], sw["alc"], sw["dx"], sw["gn"],
      conv0, s0)


def _ssd_dec_kernel(xbc_ref, dtx_ref, z_ref, cw_ref, cb_ref, dtbx_ref, alx_ref, dx_ref, gn_ref, conv0_ref, s0_ref,
                    *rest, heads, width):
    o_ref, st_ref, cv_ref, y_ref = rest[-4:]
    taps = SSM_CONV
    ns = SSM_D_STATE
    xr = xbc_ref[...]
    conv = cb_ref[...] + cw_ref[taps - 1:taps, :] * xr
    for j in range(taps - 1):
        conv = conv + cw_ref[j:j + 1, :] * conv0_ref[j]
    for j in range(taps - 2):
        cv_ref[j] = conv0_ref[j + 1]
    cv_ref[taps - 2] = xr
    xbc = _silu(conv)
    xs = xbc[:, :width]
    bm = xbc[:, width:width + SSM_GROUPS * ns]
    cm = xbc[:, width + SSM_GROUPS * ns:].astype(BF16)
    dt = _softplus(dtx_ref[...] + dtbx_ref[...])
    da = jnp.exp(-jnp.exp(alx_ref[...]) * dt)
    xdt = xs * dt
    per_group = width // SSM_GROUPS
    for r in range(DEC_ROWS):
        for cidx in range(width // LANES):
            ls = slice(cidx * LANES, (cidx + 1) * LANES)
            g = (cidx * LANES) // per_group
            hs = cidx * (LANES // SSM_HEAD_DIM)
            dcol = _col_bcast(da[r:r + 1, ls])
            xcol = _col_bcast(xdt[r:r + 1, ls])
            parts = []
            for t in range(LANES // SSM_HEAD_DIM):
                sl = slice(t * SSM_HEAD_DIM, (t + 1) * SSM_HEAD_DIM)
                s_new = dcol[sl, :] * s0_ref[r, hs + t] + xcol[sl, :] * bm[r:r + 1, g * ns:(g + 1) * ns]
                st_ref[r, hs + t] = s_new
                parts.append(s_new)
            s_cat = jnp.concatenate(parts, axis=0).astype(BF16)
            crow = jnp.broadcast_to(cm[r:r + 1, g * ns:(g + 1) * ns], (8, ns))
            y_ref[r:r + 1, ls] = _dot_nt(crow, s_cat)[0:1, :]
    y = y_ref[...] + dx_ref[...] * xs
    outs = _group_norm_gate(y, z_ref[...], gn_ref[...], SSM_GROUPS)
    gw = width // SSM_GROUPS
    for g in range(SSM_GROUPS):
        o_ref[:, g * gw:(g + 1) * gw] = outs[g]


def _ssd_dec(p, dtx, blk, sw, conv_t, states, prev_new, layer):
    m = p.shape[0]
    heads = sw["dtb"].shape[1]
    width = heads * SSM_HEAD_DIM
    ch = sw["cw"].shape[1]
    full = lambda a: pl.BlockSpec(a.shape, lambda i: (0,) * a.ndim)
    st_spec = pl.BlockSpec((None, DEC_ROWS, heads, SSM_HEAD_DIM, SSM_D_STATE), lambda i: (layer, i, 0, 0, 0))
    cv_spec = pl.BlockSpec((None, SSM_CONV - 1, DEC_ROWS, ch), lambda i: (layer, 0, i, 0))
    in_specs = [pl.BlockSpec((DEC_ROWS, ch), lambda i: (i, blk["xbc"])),
                pl.BlockSpec((DEC_ROWS, width), lambda i: (i, 0)),
                pl.BlockSpec((DEC_ROWS, width), lambda i: (i, blk["z"])),
                full(sw["cw"]), full(sw["cb"]), full(sw["dtbx"]), full(sw["alx"]), full(sw["dx"]), full(sw["gn"]),
                cv_spec, st_spec]
    args = [p, dtx, p, sw["cw"], sw["cb"], sw["dtbx"], sw["alx"], sw["dx"], sw["gn"], conv_t, states]
    aliases = {}
    if prev_new is not None:
        in_specs += [pl.BlockSpec(memory_space=pl.ANY)] * 2
        aliases = {len(args): 1, len(args) + 1: 2}
        args += list(prev_new)
    return pl.pallas_call(
        functools.partial(_ssd_dec_kernel, heads=heads, width=width),
        grid=(m // DEC_ROWS,),
        in_specs=in_specs,
        out_specs=[pl.BlockSpec((DEC_ROWS, width), lambda i: (i, 0)), st_spec, cv_spec],
        out_shape=[jax.ShapeDtypeStruct((m, width), F32),
                   jax.ShapeDtypeStruct(states.shape, F32),
                   jax.ShapeDtypeStruct(conv_t.shape, F32)],
        scratch_shapes=[pltpu.VMEM((DEC_ROWS, width), F32)],
        input_output_aliases=aliases,
        compiler_params=_cparams("parallel"),
        name="ssd_dec",
    )(*args)


def _dec_q_kernel(q_ref, wk_ref, tc_ref, ta_ref, tb_ref, o_ref, *, kvl):
    qr = _rope128(q_ref[:, MLA_NOPE:].astype(F32), tc_ref[...], ta_ref[...], tb_ref[...])
    o_ref[:, :kvl] = _dot_nt(q_ref[:, :MLA_NOPE], wk_ref[...])
    o_ref[:, kvl:] = qr


def _dec_q(q, wk, tabs, heads):
    m = q.shape[0]
    kvl = wk.shape[0]
    tab_spec = pl.BlockSpec((m, LANES), lambda h: (0, 0))
    return pl.pallas_call(
        functools.partial(_dec_q_kernel, kvl=kvl),
        grid=(heads,),
        in_specs=[pl.BlockSpec((m, MLA_QPAD), lambda h: (0, h)),
                  pl.BlockSpec((kvl, MLA_NOPE), lambda h: (0, h)),
                  tab_spec, tab_spec, tab_spec],
        out_specs=pl.BlockSpec((m, kvl + LANES), lambda h: (0, h)),
        out_shape=jax.ShapeDtypeStruct((m, heads * (kvl + LANES)), F32),
        compiler_params=_cparams("parallel"),
        name="dec_q",
    )(q, wk, *tabs)


def _dec_attn_kernel(pt_ref, q_ref, lat_new_ref, kr_new_ref, lat_hbm, kr_hbm, o_ref, lat_buf, kr_buf, sem_lat, sem_kr,
                     *, layer, pg, groups, kvl, scale):
    b = pl.program_id(0)
    total = pl.num_programs(0) * groups

    def page_copies(slot, j, page):
        return (pltpu.make_async_copy(lat_hbm.at[layer, page], lat_buf.at[slot, j], sem_lat.at[slot]),
                pltpu.make_async_copy(kr_hbm.at[layer, page], kr_buf.at[slot, j], sem_kr.at[slot]))

    def start_pages(t, slot, j0, j1):
        row = t // groups
        col0 = (t % groups) * pg
        for j in range(j0, j1):
            for c in page_copies(slot, j, pt_ref[row, col0 + j]):
                c.start()

    def wait_group(slot):
        for j in range(pg):
            for c in page_copies(slot, j, 0):
                c.wait()

    @pl.when(b == 0)
    def _():
        for t0 in range(DEC_AHEAD):
            start_pages(t0, t0, 0, pg)

    q = q_ref[0]
    q_lat = q[:, :kvl]
    q_rope = q[:, kvl:kvl + MLA_ROPE]
    r = b % 8
    lat_new = lat_new_ref[pl.ds(r, 1), :]
    kr_new = kr_new_ref[pl.ds(r, 1), :]
    m0 = (jnp.sum(q_lat * lat_new, axis=1, keepdims=True)
          + jnp.sum(q_rope * kr_new, axis=1, keepdims=True)) * scale
    l0 = jnp.ones_like(m0)
    acc0 = jnp.broadcast_to(lat_new, (q.shape[0], kvl))
    qlb = q_lat.astype(BF16)
    qrb = q_rope.astype(BF16)

    def body(g, carry):
        m_prev, l_prev, acc = carry
        t = b * groups + g
        slot = t % DEC_SLOTS
        t_next = jnp.minimum(t + DEC_AHEAD, total - 1)
        s_next = (t + DEC_AHEAD) % DEC_SLOTS
        cut1, cut2 = pg // 3, (2 * pg) // 3
        start_pages(t_next, s_next, 0, cut1)
        wait_group(slot)
        lats, ss = [], []
        for j in range(pg):
            lat = lat_buf[slot, j].astype(BF16)
            kr_t = kr_buf[slot, j].astype(BF16)
            lats.append(lat)
            ss.append((_dot_nt(qlb, lat) + jnp.dot(qrb, kr_t, preferred_element_type=F32)) * scale)
        start_pages(t_next, s_next, cut1, cut2)
        m_new = m_prev
        for s in ss:
            m_new = jnp.maximum(m_new, jnp.max(s, axis=1, keepdims=True))
        corr = jnp.exp(m_prev - m_new)
        l_new = l_prev * corr
        acc = acc * corr
        for j in range(pg):
            pe = jnp.exp(ss[j] - m_new)
            l_new = l_new + jnp.sum(pe, axis=1, keepdims=True)
            acc = acc + jnp.dot(pe.astype(BF16), lats[j], preferred_element_type=F32)
        start_pages(t_next, s_next, cut2, pg)
        return m_new, l_new, acc

    _, l_fin, acc_fin = lax.fori_loop(0, groups, body, (m0, l0, acc0))
    o_ref[0] = acc_fin / l_fin

    @pl.when(b == pl.num_programs(0) - 1)
    def _():
        for extra in range(DEC_AHEAD):
            wait_group((total + extra) % DEC_SLOTS)


def _dec_attn(page_table, q3, lat_new, kr_new, cache_lat, cache_kr_t, layer):
    m, heads, qw = q3.shape
    kvl = cache_lat.shape[-1]
    n_pages = page_table.shape[1]
    pg = min(DEC_PAGES_PER_STEP, n_pages)
    groups = n_pages // pg
    assert groups * pg == n_pages
    scale = (MLA_NOPE + MLA_ROPE) ** -0.5
    return pl.pallas_call(
        functools.partial(_dec_attn_kernel, layer=layer, pg=pg, groups=groups, kvl=kvl, scale=scale),
        grid_spec=pltpu.PrefetchScalarGridSpec(
            num_scalar_prefetch=1,
            grid=(m,),
            in_specs=[pl.BlockSpec((1, heads, qw), lambda b, pt: (b, 0, 0)),
                      pl.BlockSpec((8, kvl), lambda b, pt: (b // 8, 0)),
                      pl.BlockSpec((8, MLA_ROPE), lambda b, pt: (b // 8, 0)),
                      pl.BlockSpec(memory_space=pl.ANY),
                      pl.BlockSpec(memory_space=pl.ANY)],
            out_specs=pl.BlockSpec((1, heads, kvl), lambda b, pt: (b, 0, 0)),
            scratch_shapes=[pltpu.VMEM((DEC_SLOTS, pg, PAGE_SIZE, kvl), F32),
                            pltpu.VMEM((DEC_SLOTS, pg, MLA_ROPE, PAGE_SIZE), F32),
                            pltpu.SemaphoreType.DMA((DEC_SLOTS,)),
                            pltpu.SemaphoreType.DMA((DEC_SLOTS,))]),
        out_shape=jax.ShapeDtypeStruct((m, heads, kvl), F32),
        compiler_params=_cparams("arbitrary"),
        name="dec_attn",
    )(page_table, q3, lat_new, kr_new, cache_lat, cache_kr_t)


def _dec_ov_kernel(o_ref, wv_ref, gate_ref, out_ref):
    out_ref[...] = (jnp.dot(o_ref[...].astype(BF16), wv_ref[...], preferred_element_type=F32)
                    * _silu(gate_ref[...]))


def _dec_ov(o_lat2, wv, p, gate_blk0, heads):
    m = o_lat2.shape[0]
    kvl = wv.shape[0]
    return pl.pallas_call(
        _dec_ov_kernel,
        grid=(heads,),
        in_specs=[pl.BlockSpec((m, kvl), lambda h: (0, h)),
                  pl.BlockSpec((kvl, MLA_V), lambda h: (0, h)),
                  pl.BlockSpec((m, MLA_V), lambda h: (0, gate_blk0 + h))],
        out_specs=pl.BlockSpec((m, MLA_V), lambda h: (0, h)),
        out_shape=jax.ShapeDtypeStruct((m, heads * MLA_V), F32),
        compiler_params=_cparams("parallel"),
        name="dec_ov",
    )(o_lat2, wv, p)


def _post_kernel(y_ref, h_ref, g_ref, o_ref):
    y = y_ref[...]
    ms = jnp.mean(y * y, axis=-1, keepdims=True)
    o_ref[...] = h_ref[...] + y * lax.rsqrt(ms + EPS) * g_ref[...]


def _post(y, h, g, bm):
    m, d = y.shape
    return pl.pallas_call(
        _post_kernel,
        grid=(m // bm,),
        in_specs=[pl.BlockSpec((bm, d), lambda i: (i, 0)),
                  pl.BlockSpec((bm, d), lambda i: (i, 0)),
                  pl.BlockSpec((1, d), lambda i: (0, 0))],
        out_specs=pl.BlockSpec((bm, d), lambda i: (i, 0)),
        out_shape=jax.ShapeDtypeStruct((m, d), F32),
        compiler_params=_cparams("parallel"),
        name="post_norm",
    )(y, h, g)


def _layout(segments):
    off, offs = 0, {}
    for name, width, block in segments:
        off = -(-off // block) * block
        offs[name] = off
        off += width
    return offs, -(-off // LANES) * LANES


def _rope_tables(pos):
    half = MLA_ROPE // 2
    inv_freq = ROPE_THETA ** (-jnp.arange(half, dtype=F32) / half)
    ang = pos.astype(F32)[:, None] * inv_freq[None, :]
    cos, sin = jnp.cos(ang), jnp.sin(ang)
    zeros = jnp.zeros_like(cos)
    pad = jnp.zeros((pos.shape[0], LANES - MLA_ROPE), F32)
    tc = jnp.concatenate([cos, cos, pad], axis=1)
    ta = jnp.concatenate([-sin, zeros, pad], axis=1)
    tb = jnp.concatenate([zeros, sin, pad], axis=1)
    return tc, ta, tb


def _pick(n, prefs):
    for b in prefs:
        if n % b == 0:
            return b
    return n


def kernel(x_prompt, x_sample, cache_mla_latent, cache_mla_krope, state_gla, state_ssm, state_conv, page_table,
           norm_pre, norm_post, w_in, mla_q_norm, mla_w_qb, mla_kv_norm, mla_w_kb, mla_w_vb, gla_w_a2, gla_b_a,
           gla_norm, ssm_conv_w, ssm_conv_b, ssm_dt_bias, ssm_a_log, ssm_d, ssm_norm, w_out):
    depth = w_in.shape[0]
    bp, lp, d = x_prompt.shape
    bs, ls, _ = x_sample.shape
    assert ls == 1
    q_lora = mla_q_norm.shape[1]
    kvl = mla_kv_norm.shape[1]
    heads = mla_w_kb.shape[2]
    dv = gla_norm.shape[1]
    dk = gla_w_a2.shape[2] // GLA_HEADS
    s_heads = ssm_d.shape[1]
    s_width = s_heads * SSM_HEAD_DIM
    conv_ch = ssm_conv_w.shape[2]
    mla_w = heads * MLA_V
    gla_w = GLA_HEADS * dv
    assert dk == LANES and dv % LANES == 0 and kvl % LANES == 0

    in_sizes = (q_lora, kvl + MLA_ROPE, GLA_HEADS * dk, GLA_HEADS * dk, gla_w, gla_w_a2.shape[1], conv_ch, s_heads,
                mla_w, gla_w, s_width)
    names = ("q_a", "kv_a", "g_q", "g_k", "g_v", "g_a", "xbc", "dt", "gate_mla", "gate_gla", "z")
    src, acc = {}, 0
    for nm, sz in zip(names, in_sizes):
        src[nm] = (acc, sz)
        acc += sz
    segs = [("q_a", q_lora, q_lora), ("z", s_width, s_width), ("g_v", gla_w, gla_w), ("gate_gla", gla_w, gla_w),
            ("gate_mla", mla_w, MLA_V), ("xbc", conv_ch, conv_ch), ("ckv", kvl, kvl),
            ("g_q", GLA_HEADS * dk, GLA_HEADS * dk), ("g_k", GLA_HEADS * dk, GLA_HEADS * dk)]
    offs, n_main = _layout(segs)
    blk = {"q_a": offs["q_a"] // q_lora, "z": offs["z"] // s_width, "g_v": offs["g_v"] // dv,
           "gate_gla": offs["gate_gla"] // dv, "gate_mla": offs["gate_mla"] // MLA_V,
           "xbc": offs["xbc"] // conv_ch, "ckv": offs["ckv"] // kvl, "g_q": offs["g_q"] // dk,
           "g_k": offs["g_k"] // dk}

    def w_in_main_t(wl_t):
        parts, end = [], 0
        for nm, width, _ in segs:
            if offs[nm] > end:
                parts.append(jnp.zeros((offs[nm] - end, d), F32))
            s0 = src["kv_a"][0] if nm == "ckv" else src[nm][0]
            parts.append(wl_t[s0:s0 + width])
            end = offs[nm] + width
        if n_main > end:
            parts.append(jnp.zeros((n_main - end, d), F32))
        return jnp.concatenate(parts, axis=0).astype(BF16)

    def w_in_small_t(wl_t):
        kr0 = src["kv_a"][0] + kvl
        parts = [wl_t[kr0:kr0 + MLA_ROPE], wl_t[src["g_a"][0]:src["g_a"][0] + src["g_a"][1]],
                 wl_t[src["dt"][0]:src["dt"][0] + s_heads]]
        assert src["g_a"][1] == SMALL_DT - SMALL_GA and SMALL_DT + s_heads <= SMALL_W
        used = MLA_ROPE + src["g_a"][1] + s_heads
        parts.append(jnp.zeros((SMALL_W - used, d), F32))
        return jnp.concatenate(parts, axis=0).astype(BF16)

    mp, ms_ = bp * lp, bs
    n_pages = page_table.shape[1]
    tabs_p = _rope_tables(jnp.arange(lp))
    tabs_s = _rope_tables(jnp.full((ms_,), n_pages * PAGE_SIZE))
    hp = x_prompt.reshape(mp, d)
    hs = x_sample.reshape(ms_, d)
    bn_main = _pick(n_main, (1024, 512, 256, 128))
    bn_out = _pick(d, (1024, 512, 256, 128))
    bm_p = _pick(mp, (1024, 512, 256))
    bm_row = _pick(mp, (256,))
    bm_kv = _pick(lp, (256, 128))
    conv0_p = jnp.zeros((bp, SSM_CONV - 1, conv_ch), F32)
    ssm0_p = jnp.zeros((bp, s_heads, SSM_HEAD_DIM, SSM_D_STATE), F32)
    cache_kr_t = jnp.swapaxes(cache_mla_krope, 2, 3)
    conv_t = jnp.swapaxes(state_conv, 1, 2)
    gla_new_s = jnp.zeros(state_gla.shape, F32)
    ssm_new_s = jnp.zeros(state_ssm.shape, F32)
    conv_new_s = jnp.zeros(conv_t.shape, F32)

    rows_p, rows_s = [], []
    for l in range(depth):
        wl = w_in[l]
        wl_t = jnp.swapaxes(wl, 0, 1)
        w_main = w_in_main_t(wl_t)
        w_small = w_in_small_t(wl_t)
        w_qb = jnp.pad(mla_w_qb[l].reshape(q_lora, heads, MLA_NOPE + MLA_ROPE),
                       ((0, 0), (0, 0), (0, MLA_QPAD - MLA_NOPE - MLA_ROPE))).reshape(q_lora, heads * MLA_QPAD)
        w_qb = w_qb.astype(BF16)
        w_kb = mla_w_kb[l].reshape(kvl, heads * MLA_NOPE).astype(BF16)
        w_vb = mla_w_vb[l].reshape(kvl, heads * MLA_V).astype(BF16)
        w_o = w_out[l].astype(BF16)
        wa_pad = jnp.zeros((SMALL_W, GLA_HEADS * dk), F32)
        wa_pad = lax.dynamic_update_slice(wa_pad, gla_w_a2[l], (SMALL_GA, 0)).astype(BF16)
        ba = gla_b_a[l][None, :]
        gn_gla = gla_norm[l][None, :]
        sw = {"cw": ssm_conv_w[l], "cb": ssm_conv_b[l][None, :], "dtb": ssm_dt_bias[l][None, :],
              "dtbc": ssm_dt_bias[l][:, None], "al": ssm_a_log[l][None, :], "alc": ssm_a_log[l][:, None],
              "dx": jnp.repeat(ssm_d[l], SSM_HEAD_DIM)[None, :], "gn": ssm_norm[l][None, :],
              "dtbx": jnp.repeat(ssm_dt_bias[l], SSM_HEAD_DIM)[None, :],
              "alx": jnp.repeat(ssm_a_log[l], SSM_HEAD_DIM)[None, :]}
        g_pre = norm_pre[l][None, :]
        g_post = norm_post[l][None, :]
        g_q = mla_q_norm[l][None, :]
        g_kv = mla_kv_norm[l][None, :]

        u, small = _norm_small(hp, g_pre, w_small, bm_row)
        p = _matmul_nt(u, w_main, bm_p, bn_main)
        q = _norm_matmul(p, blk["q_a"], g_q, w_qb, _pick(mp, (1024, 512, 256)),
                         _pick(heads * MLA_QPAD, (1024, 512, 256)))
        lat, kr, kcat, v = _mla_kv(p, blk["ckv"], small, g_kv, tabs_p, w_kb, w_vb, bm_kv, True)
        o_mla = _flash(q, kcat, v, p, blk["gate_mla"], tabs_p, bp, lp, heads)
        o_gla, gla_new = _gla_prompt(p, small, blk, wa_pad, ba, gn_gla, bp, lp, dk, dv)
        dtt = small[:, SMALL_DT:SMALL_DT + s_heads].reshape(bp, lp, s_heads).transpose(0, 2, 1)
        o_ssm, ssm_new, conv_new = _ssd_prompt(p, small, dtt, blk, sw, conv0_p, ssm0_p, bp, lp)
        y = _matmul([o_mla, o_gla, o_ssm], w_o, [mla_w, gla_w, s_width], bm_p, bn_out)
        hp = _post(y, hp, g_post, bm_row)
        rows_p.append((lat.reshape(bp, lp, kvl), kr.reshape(bp, lp, MLA_ROPE), gla_new, ssm_new, conv_new))

        u, small = _norm_small(hs, g_pre, w_small, ms_)
        p = _matmul_nt(u, w_main, ms_, bn_main)
        q = _norm_matmul(p, blk["q_a"], g_q, w_qb, ms_, _pick(heads * MLA_QPAD, (1024, 512, 256)))
        lat, kr = _mla_kv(p, blk["ckv"], small, g_kv, tabs_s, w_kb, w_vb, ms_, False)
        qcat = _dec_q(q, w_kb, tabs_s, heads).reshape(ms_, heads, kvl + LANES)
        o_lat = _dec_attn(page_table, qcat, lat, kr, cache_mla_latent, cache_kr_t, l)
        o_mla = _dec_ov(o_lat.reshape(ms_, heads * kvl), w_vb, p, blk["gate_mla"], heads)
        o_gla, gla_new_s = _gla_dec(p, small, blk, wa_pad, ba, gn_gla, state_gla, gla_new_s, l, dk, dv)
        dtx = jnp.repeat(small[:, SMALL_DT:SMALL_DT + s_heads], SSM_HEAD_DIM, axis=1)
        o_ssm, ssm_new_s, conv_new_s = _ssd_dec(p, dtx, blk, sw, conv_t, state_ssm, (ssm_new_s, conv_new_s), l)
        y = _matmul([o_mla, o_gla, o_ssm], w_o, [mla_w, gla_w, s_width], ms_, bn_out)
        hs = _post(y, hs, g_post, ms_)
        rows_s.append((lat.reshape(bs, 1, kvl), kr.reshape(bs, 1, MLA_ROPE)))

    outs = [hp.reshape(bp, lp, d), hs.reshape(bs, 1, d)]
    for i in range(5):
        outs.append(jnp.stack([r[i] for r in rows_p]))
    for i in range(2):
        outs.append(jnp.stack([r[i] for r in rows_s]))
    outs += [gla_new_s, ssm_new_s, jnp.swapaxes(conv_new_s, 1, 2)]
    return tuple(outs)
```
